```python
import jax, jax.numpy as jnp
from jax import lax
import numpy as np

D_MODEL = 1024
BATCH = 8
SEQ = 4096
DEPTH = 4

N_MIXERS = 2
N_POOL_LAYERS = (DEPTH + N_MIXERS - 1) // N_MIXERS
N_HGRN_LAYERS = DEPTH // N_MIXERS
POOL_WINDOWS = (2, 4, 8, 16)
POOL_GROUPS = len(POOL_WINDOWS)
POOL_GROUP_WIDTH = D_MODEL // POOL_GROUPS
HGRN_EXPAND = 128
HGRN_HEADS = D_MODEL // HGRN_EXPAND
HGRN_HEAD_DIM = D_MODEL // HGRN_HEADS
HGRN_CHUNK = 32
D_FF = ((8 * D_MODEL // 3 + 255) // 256) * 256
EPS = 1e-6

kernel_name = "hybrid_pool_hgrn2_adaln_trunk"


def _rmsnorm(x, gain):
    xf = x.astype(jnp.float32)
    y = xf * lax.rsqrt(jnp.mean(xf * xf, axis=-1, keepdims=True) + EPS)
    return (y * gain.astype(jnp.float32)).astype(x.dtype)


def _modulate(h, shift, scale):
    return h * (1 + scale[:, None, :]) + shift[:, None, :]


def _pool_mixer(h, w_grp, ch_scale):
    B, T, D = h.shape
    hf = h.astype(jnp.float32)
    cs = jnp.cumsum(hf, axis=1)
    pos = jnp.arange(T)
    outs = []
    for g, w in enumerate(POOL_WINDOWS):
        sl = slice(g * POOL_GROUP_WIDTH, (g + 1) * POOL_GROUP_WIDTH)
        csg = cs[..., sl]
        lag = jnp.pad(csg, ((0, 0), (w, 0), (0, 0)))[:, :T]
        cnt = jnp.minimum(pos + 1, w).astype(jnp.float32)[None, :, None]
        outs.append((csg - lag) / cnt - hf[..., sl])
    d = jnp.stack(outs, axis=2)
    y = jnp.einsum('btgc,gce->btge', d, w_grp.astype(jnp.float32)).reshape(B, T, D)
    return (y * ch_scale.astype(jnp.float32)).astype(h.dtype)


def _gla_chunk_scan(q, k, v, log_f):
    NC, B, H, C, DK = q.shape
    DV = v.shape[-1]
    causal = jnp.tril(jnp.ones((C, C), dtype=bool))[:, :, None]

    def step(S, inp):
        qc, kc, vc, gc = inp
        b = jnp.cumsum(gc, axis=-2)
        o_inter = jnp.einsum('bhik,bhkv->bhiv', qc * jnp.exp(b), S)
        rel = b[:, :, :, None, :] - b[:, :, None, :, :]
        decay = jnp.where(causal, jnp.exp(jnp.where(causal, rel, 0.0)), 0.0)
        att = jnp.einsum('bhik,bhjk,bhijk->bhij', qc, kc, decay)
        o_intra = jnp.einsum('bhij,bhjv->bhiv', att, vc)
        b_last = b[:, :, -1:, :]
        S_new = jnp.exp(b_last[:, :, 0, :])[..., None] * S + jnp.einsum(
            'bhjk,bhjv->bhkv', kc * jnp.exp(b_last - b), vc)
        return S_new, o_inter + o_intra

    S0 = jnp.zeros((B, H, DK, DV), dtype=jnp.float32)
    _, o = lax.scan(step, S0, (q, k, v, log_f))
    return o


def _hgrn2_mixer(h, w_in, lb, norm_gain, w_out):
    B, T, D = h.shape
    NC = T // HGRN_CHUNK
    proj = h @ w_in
    q, z, v, og = jnp.split(proj, 4, axis=-1)
    zf = z.astype(jnp.float32)
    log_f = jax.nn.log_sigmoid(zf) + jnp.log1p(lb * jnp.exp(-zf))
    k = (1 - lb) * jax.nn.sigmoid(-zf)

    def heads(a):
        return a.astype(jnp.float32).reshape(B, NC, HGRN_CHUNK, HGRN_HEADS, -1).transpose(1, 0, 3, 2, 4)

    o = _gla_chunk_scan(heads(q), heads(k), heads(v), heads(log_f))
    o = o.transpose(1, 0, 3, 2, 4).reshape(B, T, HGRN_HEADS, HGRN_HEAD_DIM)
    o = o * lax.rsqrt(jnp.mean(o * o, axis=-1, keepdims=True) + EPS)
    o = o.reshape(B, T, D) * norm_gain.astype(jnp.float32)
    o = o * jax.nn.silu(og.astype(jnp.float32))
    return (o.astype(h.dtype)) @ w_out


def _swiglu(h, w_in, w_out):
    gu = h @ w_in
    a, b = jnp.split(gu, 2, axis=-1)
    return (jax.nn.silu(a) * b) @ w_out


def setup_inputs(seed: int = 0) -> dict:
    key = jax.random.key(seed)
    ks = jax.random.split(key, 16)
    D, F = D_MODEL, D_FF
    nrm = jax.random.normal
    return {
        "x": nrm(ks[0], (BATCH, SEQ, D), jnp.float32),
        "c": nrm(ks[1], (BATCH, D), jnp.float32),
        "norm_mix_gain": 1.0 + 0.02 * nrm(ks[2], (DEPTH, D), jnp.float32),
        "norm_ffn_gain": 1.0 + 0.02 * nrm(ks[3], (DEPTH, D), jnp.float32),
        "ada_w": 0.5 * D ** -0.5 * nrm(ks[4], (DEPTH, D, 6 * D), jnp.float32),
        "ada_b": 0.02 * nrm(ks[5], (DEPTH, 6 * D), jnp.float32),
        "pool_w": POOL_GROUP_WIDTH ** -0.5 * nrm(ks[6], (N_POOL_LAYERS, POOL_GROUPS, POOL_GROUP_WIDTH, POOL_GROUP_WIDTH), jnp.float32),
        "pool_scale": 1.0 + 0.1 * nrm(ks[7], (N_POOL_LAYERS, D), jnp.float32),
        "hgrn_w_in": D ** -0.5 * nrm(ks[8], (N_HGRN_LAYERS, D, 4 * D), jnp.float32),
        "hgrn_lb_logits": nrm(ks[9], (N_HGRN_LAYERS, D), jnp.float32),
        "hgrn_norm_gain": 1.0 + 0.02 * nrm(ks[10], (N_HGRN_LAYERS, D), jnp.float32),
        "hgrn_w_out": D ** -0.5 * nrm(ks[11], (N_HGRN_LAYERS, D, D), jnp.float32),
        "ffn_w_in": D ** -0.5 * nrm(ks[12], (DEPTH, D, 2 * F), jnp.float32),
        "ffn_w_out": F ** -0.5 * nrm(ks[13], (DEPTH, F, D), jnp.float32),
        "final_gain": 1.0 + 0.02 * nrm(ks[14], (D,), jnp.float32),
    }


def reference(x, c, norm_mix_gain, norm_ffn_gain, ada_w, ada_b, pool_w, pool_scale,
              hgrn_w_in, hgrn_lb_logits, hgrn_norm_gain, hgrn_w_out, ffn_w_in, ffn_w_out,
              final_gain):
    sm = jax.nn.softmax(hgrn_lb_logits.astype(jnp.float32), axis=0)
    lower_bounds = jnp.cumsum(sm, axis=0) - sm[0]
    c_act = jax.nn.silu(c)
    for i in range(DEPTH):
        mod = c_act @ ada_w[i] + ada_b[i]
        sh1, sc1, g1, sh2, sc2, g2 = jnp.split(mod, 6, axis=-1)
        h = _modulate(_rmsnorm(x, norm_mix_gain[i]), sh1, sc1)
        j = i // N_MIXERS
        if i % N_MIXERS == 0:
            y = _pool_mixer(h, pool_w[j], pool_scale[j])
        else:
            y = _hgrn2_mixer(h, hgrn_w_in[j], lower_bounds[j], hgrn_norm_gain[j], hgrn_w_out[j])
        x = x + g1[:, None, :] * y
        h = _modulate(_rmsnorm(x, norm_ffn_gain[i]), sh2, sc2)
        x = x + g2[:, None, :] * _swiglu(h, ffn_w_in[i], ffn_w_out[i])
    return _rmsnorm(x, final_gain)
```

```python
import functools

import jax
import jax.numpy as jnp
from jax import lax
from jax.experimental import pallas as pl
from jax.experimental.pallas import tpu as pltpu

EPS = 1e-6
POOL_WINDOWS = (2, 4, 8, 16)
POOL_HALO = 16
HEAD_DIM = 128
CHUNK = 32
SUPER = 128
MAX_SPAN = 64.0
VMEM_LIMIT = 56 * 1024 * 1024

F32 = jnp.float32
BF16 = jnp.bfloat16


def _dot(a, b):
    return jnp.dot(a, b, preferred_element_type=F32)


def _dot_nt(a, b):
    return lax.dot_general(a, b, (((1,), (1,)), ((), ())), preferred_element_type=F32)


def _dot_tn(a, b):
    return lax.dot_general(a, b, (((0,), (0,)), ((), ())), preferred_element_type=F32)


def _const_spec(shape):
    nd = len(shape)
    return pl.BlockSpec(shape, lambda *_: (0,) * nd)


def _norm_mod(x, gain, shift, scale):
    ms = jnp.mean(x * x, axis=-1, keepdims=True)
    return (x * lax.rsqrt(ms + EPS)) * (gain * (1.0 + scale)) + shift


def _ada_kernel(c_ref, w_ref, b_ref, o_ref):
    c = c_ref[...]
    ca = (c * jax.nn.sigmoid(c)).astype(BF16)
    o_ref[...] = _dot(ca, w_ref[...].astype(BF16)) + b_ref[...]


def _ada_mod(c, ada_w, ada_b):
    depth, d, n = ada_w.shape
    bsz = c.shape[0]
    tn = n // 4
    return pl.pallas_call(
        _ada_kernel,
        grid=(depth, n // tn),
        in_specs=[
            pl.BlockSpec((bsz, d), lambda l, j: (0, 0)),
            pl.BlockSpec((None, d, tn), lambda l, j: (l, 0, j)),
            pl.BlockSpec((None, 1, tn), lambda l, j: (l, 0, j)),
        ],
        out_specs=pl.BlockSpec((None, bsz, tn), lambda l, j: (l, 0, j)),
        out_shape=jax.ShapeDtypeStruct((depth, bsz, n), F32),
        compiler_params=pltpu.CompilerParams(
            dimension_semantics=("arbitrary", "arbitrary"),
            vmem_limit_bytes=VMEM_LIMIT),
        name="ada_mod",
    )(c, ada_w, ada_b.reshape(depth, 1, n))


def _ffn_kernel(x_ref, mod_ref, gain_ref, win_ref, wout_ref, fgain_ref, o_ref, act_ref,
                *, fc, final):
    f = wout_ref.shape[0]
    m = mod_ref[...]
    h = _norm_mod(x_ref[...], gain_ref[...], m[3:4], m[4:5]).astype(BF16)
    for c in range(f // fc):
        a = _dot(h, win_ref[:, c * fc:(c + 1) * fc])
        b = _dot(h, win_ref[:, f + c * fc:f + (c + 1) * fc])
        act_ref[:, c * fc:(c + 1) * fc] = (a * jax.nn.sigmoid(a) * b).astype(BF16)
    y = _dot(act_ref[...], wout_ref[...])
    out = x_ref[...] + m[5:6] * y
    if final:
        ms = jnp.mean(out * out, axis=-1, keepdims=True)
        out = out * lax.rsqrt(ms + EPS) * fgain_ref[...]
    o_ref[...] = out


def _ffn_layer(x, mod, layer, gain, w_in, w_out, final_gain, *, final, tm=512, fc=256):
    bsz, t, d = x.shape
    f = w_out.shape[0]
    kern = functools.partial(_ffn_kernel, fc=fc, final=final)
    return pl.pallas_call(
        kern,
        grid=(bsz, t // tm),
        in_specs=[
            pl.BlockSpec((None, tm, d), lambda b, i: (b, i, 0)),
            pl.BlockSpec((None, None, 6, d), lambda b, i: (layer, b, 0, 0)),
            _const_spec((1, d)),
            _const_spec((d, 2 * f)),
            _const_spec((f, d)),
            _const_spec((1, d)),
        ],
        out_specs=pl.BlockSpec((None, tm, d), lambda b, i: (b, i, 0)),
        out_shape=jax.ShapeDtypeStruct(x.shape, F32),
        scratch_shapes=[pltpu.VMEM((tm, f), BF16)],
        compiler_params=pltpu.CompilerParams(
            dimension_semantics=("parallel", "parallel"),
            vmem_limit_bytes=VMEM_LIMIT),
        name="ffn",
    )(x, mod, gain, w_in, w_out, final_gain)


def _pool_kernel(x_ref, halo_ref, mod_ref, gain_ref, pw_ref, ps_ref, o_ref, hext_ref):
    i = pl.program_id(1)
    tt, d = x_ref.shape
    gw = d // len(POOL_WINDOWS)
    m = mod_ref[...]
    gain = gain_ref[...]
    h = _norm_mod(x_ref[...], gain, m[0:1], m[1:2])
    hh = _norm_mod(halo_ref[...], gain, m[0:1], m[1:2])
    hext_ref[0:POOL_HALO, :] = jnp.where(i == 0, 0.0, hh)
    hext_ref[POOL_HALO:, :] = h
    pos = i * tt + lax.broadcasted_iota(jnp.int32, (tt, 1), 0)
    for g, w in enumerate(POOL_WINDOWS):
        cols = slice(g * gw, (g + 1) * gw)
        hg = hext_ref[POOL_HALO:, cols]
        acc = hg
        for s in range(1, w):
            acc = acc + hext_ref[POOL_HALO - s:POOL_HALO - s + tt, cols]
        cnt = jnp.minimum(pos + 1, w).astype(F32)
        dlt = acc / cnt - hg
        y = _dot(dlt.astype(BF16), pw_ref[g]) * ps_ref[:, cols]
        o_ref[:, cols] = x_ref[:, cols] + m[2:3, cols] * y


def _pool_layer(x, mod, layer, gain, pool_w, pool_scale, *, tt=512):
    bsz, t, d = x.shape
    hb = tt // POOL_HALO
    return pl.pallas_call(
        _pool_kernel,
        grid=(bsz, t // tt),
        in_specs=[
            pl.BlockSpec((None, tt, d), lambda b, i: (b, i, 0)),
            pl.BlockSpec((None, POOL_HALO, d), lambda b, i: (b, jnp.maximum(i * hb - 1, 0), 0)),
            pl.BlockSpec((None, None, 6, d), lambda b, i: (layer, b, 0, 0)),
            _const_spec((1, d)),
            _const_spec(pool_w.shape),
            _const_spec((1, d)),
        ],
        out_specs=pl.BlockSpec((None, tt, d), lambda b, i: (b, i, 0)),
        out_shape=jax.ShapeDtypeStruct(x.shape, F32),
        scratch_shapes=[pltpu.VMEM((tt + POOL_HALO, d), F32)],
        compiler_params=pltpu.CompilerParams(
            dimension_semantics=("parallel", "parallel"),
            vmem_limit_bytes=VMEM_LIMIT),
        name="pool_mixer",
    )(x, x, mod, gain, pool_w, pool_scale)


def _lower_bound(logits, row):
    mx = jnp.max(logits, axis=0, keepdims=True)
    e = jnp.exp(logits - mx)
    sm = e / jnp.sum(e, axis=0, keepdims=True)
    acc = sm[0:1]
    for r in range(1, row + 1):
        acc = acc + sm[r:r + 1]
    return acc - sm[0:1]


def _hgrn_kernel(x_ref, mod_ref, gain_ref, lbl_ref, win_ref, hgain_ref, wout_ref, o_ref,
                 st_ref, qt_ref, kt_ref, kh_ref, v_ref, b_ref, qk_ref, dl_ref, gate_ref, oacc_ref,
                 *, row):
    i = pl.program_id(1)
    tt, d = x_ref.shape
    heads = d // HEAD_DIM
    n_chunks = tt // CHUNK

    @pl.when(i == 0)
    def _():
        st_ref[...] = jnp.zeros_like(st_ref)

    m = mod_ref[...]
    h = _norm_mod(x_ref[...], gain_ref[...], m[0:1], m[1:2]).astype(BF16)
    lb = _lower_bound(lbl_ref[...], row)

    q = _dot(h, win_ref[:, 0:d])
    z = _dot(h, win_ref[:, d:2 * d])
    v = _dot(h, win_ref[:, 2 * d:3 * d])
    v_ref[...] = v.astype(BF16)
    qk_ref[2] = v
    og = _dot(h, win_ref[:, 3 * d:4 * d])
    gate_ref[...] = og * jax.nn.sigmoid(og)

    e = jnp.exp(-z)
    s = 1.0 / (1.0 + e)
    logf = jnp.log(lb + (1.0 - lb) * s)
    k = (1.0 - lb) * (e * s)

    r = lax.broadcasted_iota(jnp.int32, (tt, tt), 0)
    c = lax.broadcasted_iota(jnp.int32, (tt, tt), 1)
    tri = jnp.where((r // CHUNK == c // CHUNK) & (c <= r), 1.0, 0.0).astype(BF16)
    g_hi = logf.astype(BF16)
    g_lo = (logf - g_hi.astype(F32)).astype(BF16)
    b = _dot(tri, g_hi) + _dot(tri, g_lo)
    b_ref[...] = b
    qk_ref[0] = q
    qk_ref[1] = k

    qt_ref[...] = (q * jnp.exp(b)).astype(BF16)
    kt_ref[...] = (k * jnp.exp(-b)).astype(BF16)
    span = jnp.zeros((1, d), F32)
    for cidx in range(n_chunks):
        rows = slice(cidx * CHUNK, (cidx + 1) * CHUNK)
        bl = b[(cidx + 1) * CHUNK - 1:(cidx + 1) * CHUNK]
        kh_ref[rows, :] = (k[rows] * jnp.exp(bl - b[rows])).astype(BF16)
        dl_ref[cidx] = jnp.exp(bl)
        span = jnp.maximum(span, -bl)
    fast = jnp.max(span) <= MAX_SPAN

    ri = lax.broadcasted_iota(jnp.int32, (SUPER, SUPER), 0)
    ci = lax.broadcasted_iota(jnp.int32, (SUPER, SUPER), 1)
    causal = (ri // CHUNK == ci // CHUNK) & (ci <= ri)

    @pl.when(fast)
    def _():
        def body(sc, carry):
            rows = pl.ds(pl.multiple_of(sc * SUPER, SUPER), SUPER)
            for hd in range(heads):
                cols = slice(hd * HEAD_DIM, (hd + 1) * HEAD_DIM)
                att = _dot_nt(qt_ref[rows, cols], kt_ref[rows, cols])
                att = jnp.where(causal, att, 0.0).astype(BF16)
                oacc_ref[rows, cols] = _dot(att, v_ref[rows, cols])
            return carry
        lax.fori_loop(0, tt // SUPER, body, 0)

    @pl.when(jnp.logical_not(fast))
    def _():
        rowi = lax.broadcasted_iota(jnp.int32, (CHUNK, 1), 0)

        def body(cidx, carry):
            r0 = pl.multiple_of(cidx * CHUNK, CHUNK)
            rows = pl.ds(r0, CHUNK)
            for hd in range(heads):
                cols = slice(hd * HEAD_DIM, (hd + 1) * HEAD_DIM)
                bb = b_ref[rows, cols]
                qq = qk_ref[0, rows, cols]
                kk = qk_ref[1, rows, cols]
                vv = qk_ref[2, rows, cols]
                acc = jnp.zeros((CHUNK, HEAD_DIM), F32)
                for j in range(CHUNK):
                    bj = bb[j:j + 1]
                    kj = kk[j:j + 1]
                    vj = vv[j:j + 1]
                    p = qq * kj * jnp.exp(jnp.minimum(bb - bj, 0.0))
                    col = jnp.sum(p, axis=-1, keepdims=True)
                    acc = acc + jnp.where(rowi >= j, col, 0.0) * vj
                oacc_ref[rows, cols] = acc
            return carry
        lax.fori_loop(0, n_chunks, body, 0)

    def body(cidx, carry):
        rows = pl.ds(pl.multiple_of(cidx * CHUNK, CHUNK), CHUNK)
        for hd in range(heads):
            cols = slice(hd * HEAD_DIM, (hd + 1) * HEAD_DIM)
            st = st_ref[hd]
            oacc_ref[rows, cols] += _dot_nt(qt_ref[rows, cols], st.astype(BF16))
            dst = _dot_tn(v_ref[rows, cols], kh_ref[rows, cols])
            st_ref[hd] = st * dl_ref[cidx, :, cols] + dst
        return carry
    lax.fori_loop(0, n_chunks, body, 0)

    hg = hgain_ref[...]
    for hd in range(heads):
        cols = slice(hd * HEAD_DIM, (hd + 1) * HEAD_DIM)
        o = oacc_ref[:, cols]
        ms = jnp.mean(o * o, axis=-1, keepdims=True)
        oacc_ref[:, cols] = o * lax.rsqrt(ms + EPS) * hg[:, cols] * gate_ref[:, cols]
    y = _dot(oacc_ref[...].astype(BF16), wout_ref[...])
    o_ref[...] = x_ref[...] + m[2:3] * y


def _hgrn_layer(x, mod, layer, gain, lb_logits, row, w_in, hgain, w_out, *, tt=256):
    bsz, t, d = x.shape
    heads = d // HEAD_DIM
    n_chunks = tt // CHUNK
    kern = functools.partial(_hgrn_kernel, row=row)
    return pl.pallas_call(
        kern,
        grid=(bsz, t // tt),
        in_specs=[
            pl.BlockSpec((None, tt, d), lambda b, i: (b, i, 0)),
            pl.BlockSpec((None, None, 6, d), lambda b, i: (layer, b, 0, 0)),
            _const_spec((1, d)),
            _const_spec(lb_logits.shape),
            _const_spec((d, 4 * d)),
            _const_spec((1, d)),
            _const_spec((d, d)),
        ],
        out_specs=pl.BlockSpec((None, tt, d), lambda b, i: (b, i, 0)),
        out_shape=jax.ShapeDtypeStruct(x.shape, F32),
        scratch_shapes=[
            pltpu.VMEM((heads, HEAD_DIM, HEAD_DIM), F32),
            pltpu.VMEM((tt, d), BF16),
            pltpu.VMEM((tt, d), BF16),
            pltpu.VMEM((tt, d), BF16),
            pltpu.VMEM((tt, d), BF16),
            pltpu.VMEM((tt, d), F32),
            pltpu.VMEM((3, tt, d), F32),
            pltpu.VMEM((n_chunks, 1, d), F32),
            pltpu.VMEM((tt, d), F32),
            pltpu.VMEM((tt, d), F32),
        ],
        compiler_params=pltpu.CompilerParams(
            dimension_semantics=("parallel", "arbitrary"),
            vmem_limit_bytes=VMEM_LIMIT),
        name="hgrn_mixer",
    )(x, mod, gain, lb_logits, w_in, hgain, w_out)


def kernel(x, c, norm_mix_gain, norm_ffn_gain, ada_w, ada_b, pool_w, pool_scale, hgrn_w_in,
           hgrn_lb_logits, hgrn_norm_gain, hgrn_w_out, ffn_w_in, ffn_w_out, final_gain):
    depth = ada_w.shape[0]
    bsz, _, d = x.shape
    n_mixers = 2
    mod = _ada_mod(c, ada_w, ada_b).reshape(depth, bsz, 6, d)
    fgain = final_gain.reshape(1, d)
    for i in range(depth):
        j = i // n_mixers
        gain = norm_mix_gain[i].reshape(1, d)
        if i % n_mixers == 0:
            x = _pool_layer(x, mod, i, gain, pool_w[j].astype(BF16), pool_scale[j].reshape(1, d))
        else:
            x = _hgrn_layer(x, mod, i, gain, hgrn_lb_logits, j, hgrn_w_in[j].astype(BF16),
                            hgrn_norm_gain[j].reshape(1, d), hgrn_w_out[j].astype(BF16))
        x = _ffn_layer(x, mod, i, norm_ffn_gain[i].reshape(1, d), ffn_w_in[i].astype(BF16),
                       ffn_w_out[i].astype(BF16), fgain, final=(i == depth - 1))
    return x
```

```python
import functools

import jax
import jax.numpy as jnp
from jax import lax
from jax.experimental import pallas as pl
from jax.experimental.pallas import tpu as pltpu

EPS = 1e-6
POOL_WINDOWS = (2, 4, 8, 16)
POOL_HALO = 16
POOL_BLOCK = 128
POOL_BAND = 256
HEAD_DIM = 128
CHUNK = 32
SUPER = 128
MAX_SPAN = 64.0
VMEM_LIMIT = 56 * 1024 * 1024

F32 = jnp.float32
BF16 = jnp.bfloat16


def _dot(a, b):
    return jnp.dot(a, b, preferred_element_type=F32)


def _dot_nt(a, b):
    return lax.dot_general(a, b, (((1,), (1,)), ((), ())), preferred_element_type=F32)


def _dot_tn(a, b):
    return lax.dot_general(a, b, (((0,), (0,)), ((), ())), preferred_element_type=F32)


def _const_spec(shape):
    nd = len(shape)
    return pl.BlockSpec(shape, lambda *_: (0,) * nd)


def _norm_mod(x, gain, shift, scale):
    ms = jnp.mean(x * x, axis=-1, keepdims=True)
    return (x * lax.rsqrt(ms + EPS)) * (gain * (1.0 + scale)) + shift


def _ada_kernel(c_ref, w_ref, b_ref, o_ref):
    c = c_ref[...]
    ca = (c * jax.nn.sigmoid(c)).astype(BF16)
    o_ref[...] = _dot(ca, w_ref[...].astype(BF16)) + b_ref[...]


def _ada_mod(c, ada_w, ada_b):
    depth, d, n = ada_w.shape
    bsz = c.shape[0]
    tn = n // 4
    return pl.pallas_call(
        _ada_kernel,
        grid=(depth, n // tn),
        in_specs=[
            pl.BlockSpec((bsz, d), lambda l, j: (0, 0)),
            pl.BlockSpec((None, d, tn), lambda l, j: (l, 0, j)),
            pl.BlockSpec((None, 1, tn), lambda l, j: (l, 0, j)),
        ],
        out_specs=pl.BlockSpec((None, bsz, tn), lambda l, j: (l, 0, j)),
        out_shape=jax.ShapeDtypeStruct((depth, bsz, n), F32),
        compiler_params=pltpu.CompilerParams(
            dimension_semantics=("arbitrary", "arbitrary"),
            vmem_limit_bytes=VMEM_LIMIT),
        name="ada_mod",
    )(c, ada_w, ada_b.reshape(depth, 1, n))


def _ffn_kernel(x_ref, mod_ref, gain_ref, win_ref, wout_ref, fgain_ref, o_ref, act_ref,
                *, fc, final):
    f = wout_ref.shape[0]
    m = mod_ref[...]
    h = _norm_mod(x_ref[...], gain_ref[...], m[3:4], m[4:5]).astype(BF16)
    for c in range(f // fc):
        a = _dot(h, win_ref[:, c * fc:(c + 1) * fc])
        b = _dot(h, win_ref[:, f + c * fc:f + (c + 1) * fc])
        act_ref[:, c * fc:(c + 1) * fc] = (a * jax.nn.sigmoid(a) * b).astype(BF16)
    y = _dot(act_ref[...], wout_ref[...])
    out = x_ref[...] + m[5:6] * y
    if final:
        ms = jnp.mean(out * out, axis=-1, keepdims=True)
        out = out * lax.rsqrt(ms + EPS) * fgain_ref[...]
    o_ref[...] = out


def _ffn_layer(x, mod, layer, gain, w_in, w_out, final_gain, *, final, tm=512, fc=256):
    bsz, t, d = x.shape
    f = w_out.shape[0]
    kern = functools.partial(_ffn_kernel, fc=fc, final=final)
    return pl.pallas_call(
        kern,
        grid=(bsz, t // tm),
        in_specs=[
            pl.BlockSpec((None, tm, d), lambda b, i: (b, i, 0)),
            pl.BlockSpec((None, None, 6, d), lambda b, i: (layer, b, 0, 0)),
            _const_spec((1, d)),
            _const_spec((d, 2 * f)),
            _const_spec((f, d)),
            _const_spec((1, d)),
        ],
        out_specs=pl.BlockSpec((None, tm, d), lambda b, i: (b, i, 0)),
        out_shape=jax.ShapeDtypeStruct(x.shape, F32),
        scratch_shapes=[pltpu.VMEM((tm, f), BF16)],
        compiler_params=pltpu.CompilerParams(
            dimension_semantics=("parallel", "parallel"),
            vmem_limit_bytes=VMEM_LIMIT),
        name="ffn",
    )(x, mod, gain, w_in, w_out, final_gain)


def _pool_kernel(x_ref, halo_ref, mod_ref, gain_ref, band_ref, pw_ref, ps_ref, o_ref,
                 hf_ref, hi_ref, lo_ref, d_ref):
    i = pl.program_id(1)
    tt, d = x_ref.shape
    gw = d // len(POOL_WINDOWS)
    m = mod_ref[...]
    gain = gain_ref[...]

    def split(a):
        hi = a.astype(BF16)
        return hi, (a - hi.astype(F32)).astype(BF16)

    h = _norm_mod(x_ref[...], gain, m[0:1], m[1:2])
    hf_ref[...] = h
    hh = jnp.where(i == 0, 0.0, _norm_mod(halo_ref[...], gain, m[0:1], m[1:2]))
    for ref, top, body in zip((hi_ref, lo_ref), split(hh), split(h)):
        ref[0:POOL_HALO, :] = top
        ref[POOL_HALO:POOL_HALO + tt, :] = body
        ref[POOL_HALO + tt:, :] = jnp.zeros((ref.shape[0] - POOL_HALO - tt, d), BF16)
    pos = i * tt + lax.broadcasted_iota(jnp.int32, (tt, 1), 0)
    for g, w in enumerate(POOL_WINDOWS):
        cols = slice(g * gw, (g + 1) * gw)
        inv_cnt = 1.0 / jnp.minimum(pos + 1, w).astype(F32)
        band = band_ref[g]
        for blk in range(tt // POOL_BLOCK):
            rows = slice(blk * POOL_BLOCK, (blk + 1) * POOL_BLOCK)
            win = slice(blk * POOL_BLOCK, blk * POOL_BLOCK + POOL_BAND)
            s = _dot(band, hi_ref[win, cols]) + _dot(band, lo_ref[win, cols])
            d_ref[rows, :] = (s * inv_cnt[rows] - hf_ref[rows, cols]).astype(BF16)
        y = _dot(d_ref[...], pw_ref[g]) * ps_ref[:, cols]
        o_ref[:, cols] = x_ref[:, cols] + m[2:3, cols] * y


def _pool_bands():
    i = jnp.arange(POOL_BLOCK)[:, None] + POOL_HALO
    j = jnp.arange(POOL_BAND)[None, :]
    return jnp.stack([((j <= i) & (j > i - w)) for w in POOL_WINDOWS]).astype(BF16)


def _pool_layer(x, mod, layer, gain, pool_w, pool_scale, *, tt=512):
    bsz, t, d = x.shape
    gw = d // len(POOL_WINDOWS)
    hb = tt // POOL_HALO
    ext = tt - POOL_BLOCK + POOL_BAND
    bands = _pool_bands()
    return pl.pallas_call(
        _pool_kernel,
        grid=(bsz, t // tt),
        in_specs=[
            pl.BlockSpec((None, tt, d), lambda b, i: (b, i, 0)),
            pl.BlockSpec((None, POOL_HALO, d), lambda b, i: (b, jnp.maximum(i * hb - 1, 0), 0)),
            pl.BlockSpec((None, None, 6, d), lambda b, i: (layer, b, 0, 0)),
            _const_spec((1, d)),
            _const_spec(bands.shape),
            _const_spec(pool_w.shape),
            _const_spec((1, d)),
        ],
        out_specs=pl.BlockSpec((None, tt, d), lambda b, i: (b, i, 0)),
        out_shape=jax.ShapeDtypeStruct(x.shape, F32),
        scratch_shapes=[
            pltpu.VMEM((tt, d), F32),
            pltpu.VMEM((ext, d), BF16),
            pltpu.VMEM((ext, d), BF16),
            pltpu.VMEM((tt, gw), BF16),
        ],
        compiler_params=pltpu.CompilerParams(
            dimension_semantics=("parallel", "parallel"),
            vmem_limit_bytes=VMEM_LIMIT),
        name="pool_mixer",
    )(x, x, mod, gain, bands, pool_w, pool_scale)


def _lower_bound(logits, row):
    mx = jnp.max(logits, axis=0, keepdims=True)
    e = jnp.exp(logits - mx)
    sm = e / jnp.sum(e, axis=0, keepdims=True)
    acc = sm[0:1]
    for r in range(1, row + 1):
        acc = acc + sm[r:r + 1]
    return acc - sm[0:1]


def _hgrn_kernel(x_ref, mod_ref, gain_ref, lbl_ref, win_ref, hgain_ref, wout_ref, o_ref,
                 st_ref, qt_ref, kt_ref, kh_ref, v_ref, b_ref, qk_ref, dl_ref, gate_ref, oacc_ref,
                 *, row):
    i = pl.program_id(1)
    tt, d = x_ref.shape
    head_cols = [slice(hd * HEAD_DIM, (hd + 1) * HEAD_DIM) for hd in range(d // HEAD_DIM)]
    n_chunks = tt // CHUNK

    @pl.when(i == 0)
    def _():
        st_ref[...] = jnp.zeros_like(st_ref)

    m = mod_ref[...]
    h = _norm_mod(x_ref[...], gain_ref[...], m[0:1], m[1:2]).astype(BF16)
    lb = _lower_bound(lbl_ref[...], row)

    q = _dot(h, win_ref[:, 0:d])
    z = _dot(h, win_ref[:, d:2 * d])
    v = _dot(h, win_ref[:, 2 * d:3 * d])
    v_ref[...] = v.astype(BF16)
    qk_ref[2] = v
    og = _dot(h, win_ref[:, 3 * d:4 * d])
    gate_ref[...] = og * jax.nn.sigmoid(og)

    e = jnp.exp(-z)
    s = 1.0 / (1.0 + e)
    logf = jnp.log(lb + (1.0 - lb) * s)
    k = (1.0 - lb) * (e * s)

    r = lax.broadcasted_iota(jnp.int32, (tt, tt), 0)
    c = lax.broadcasted_iota(jnp.int32, (tt, tt), 1)
    tri = jnp.where((r // CHUNK == c // CHUNK) & (c <= r), 1.0, 0.0).astype(BF16)
    g_hi = logf.astype(BF16)
    g_lo = (logf - g_hi.astype(F32)).astype(BF16)
    b = _dot(tri, g_hi) + _dot(tri, g_lo)
    b_ref[...] = b
    qk_ref[0] = q
    qk_ref[1] = k

    qt_ref[...] = (q * jnp.exp(b)).astype(BF16)
    kt_ref[...] = (k * jnp.exp(-b)).astype(BF16)
    span = jnp.zeros((1, d), F32)
    for cidx in range(n_chunks):
        rows = slice(cidx * CHUNK, (cidx + 1) * CHUNK)
        bl = b[(cidx + 1) * CHUNK - 1:(cidx + 1) * CHUNK]
        kh_ref[rows, :] = (k[rows] * jnp.exp(bl - b[rows])).astype(BF16)
        dl_ref[cidx] = jnp.exp(bl)
        span = jnp.maximum(span, -bl)
    fast = jnp.max(span) <= MAX_SPAN

    ri = lax.broadcasted_iota(jnp.int32, (SUPER, SUPER), 0)
    ci = lax.broadcasted_iota(jnp.int32, (SUPER, SUPER), 1)
    causal = (ri // CHUNK == ci // CHUNK) & (ci <= ri)

    @pl.when(fast)
    def _():
        for sc in range(tt // SUPER):
            rows = slice(sc * SUPER, (sc + 1) * SUPER)
            atts = [_dot_nt(qt_ref[rows, cols], kt_ref[rows, cols]) for cols in head_cols]
            for cols, att in zip(head_cols, atts):
                att = jnp.where(causal, att, 0.0).astype(BF16)
                oacc_ref[rows, cols] = _dot(att, v_ref[rows, cols])

    @pl.when(jnp.logical_not(fast))
    def _():
        rowi = lax.broadcasted_iota(jnp.int32, (CHUNK, 1), 0)

        def body(cidx, carry):
            r0 = pl.multiple_of(cidx * CHUNK, CHUNK)
            rows = pl.ds(r0, CHUNK)
            for cols in head_cols:
                bb = b_ref[rows, cols]
                qq = qk_ref[0, rows, cols]
                kk = qk_ref[1, rows, cols]
                vv = qk_ref[2, rows, cols]
                acc = jnp.zeros((CHUNK, HEAD_DIM), F32)
                for j in range(CHUNK):
                    bj = bb[j:j + 1]
                    kj = kk[j:j + 1]
                    vj = vv[j:j + 1]
                    p = qq * kj * jnp.exp(jnp.minimum(bb - bj, 0.0))
                    col = jnp.sum(p, axis=-1, keepdims=True)
                    acc = acc + jnp.where(rowi >= j, col, 0.0) * vj
                oacc_ref[rows, cols] = acc
            return carry
        lax.fori_loop(0, n_chunks, body, 0)

    for cidx in range(n_chunks):
        rows = slice(cidx * CHUNK, (cidx + 1) * CHUNK)
        for hd, cols in enumerate(head_cols):
            oacc_ref[rows, cols] += _dot_nt(qt_ref[rows, cols], st_ref[hd].astype(BF16))
        for hd, cols in enumerate(head_cols):
            dst = _dot_tn(v_ref[rows, cols], kh_ref[rows, cols])
            st_ref[hd] = st_ref[hd] * dl_ref[cidx, :, cols] + dst

    hg = hgain_ref[...]
    for cols in head_cols:
        o = oacc_ref[:, cols]
        ms = jnp.mean(o * o, axis=-1, keepdims=True)
        oacc_ref[:, cols] = o * lax.rsqrt(ms + EPS) * hg[:, cols] * gate_ref[:, cols]
    y = _dot(oacc_ref[...].astype(BF16), wout_ref[...])
    o_ref[...] = x_ref[...] + m[2:3] * y


def _hgrn_layer(x, mod, layer, gain, lb_logits, row, w_in, hgain, w_out, *, tt=256):
    bsz, t, d = x.shape
    heads = d // HEAD_DIM
    n_chunks = tt // CHUNK
    kern = functools.partial(_hgrn_kernel, row=row)
    return pl.pallas_call(
        kern,
        grid=(bsz, t // tt),
        in_specs=[
            pl.BlockSpec((None, tt, d), lambda b, i: (b, i, 0)),
            pl.BlockSpec((None, None, 6, d), lambda b, i: (layer, b, 0, 0)),
            _const_spec((1, d)),
            _const_spec(lb_logits.shape),
            _const_spec((d, 4 * d)),
            _const_spec((1, d)),
            _const_spec((d, d)),
        ],
        out_specs=pl.BlockSpec((None, tt, d), lambda b, i: (b, i, 0)),
        out_shape=jax.ShapeDtypeStruct(x.shape, F32),
        scratch_shapes=[
            pltpu.VMEM((heads, HEAD_DIM, HEAD_DIM), F32),
            pltpu.VMEM((tt, d), BF16),
            pltpu.VMEM((tt, d), BF16),
            pltpu.VMEM((tt, d), BF16),
            pltpu.VMEM((tt, d), BF16),
            pltpu.VMEM((tt, d), F32),
            pltpu.VMEM((3, tt, d), F32),
            pltpu.VMEM((n_chunks, 1, d), F32),
            pltpu.VMEM((tt, d), F32),
            pltpu.VMEM((tt, d), F32),
        ],
        compiler_params=pltpu.CompilerParams(
            dimension_semantics=("parallel", "arbitrary"),
            vmem_limit_bytes=VMEM_LIMIT),
        name="hgrn_mixer",
    )(x, mod, gain, lb_logits, w_in, hgain, w_out)


def kernel(x, c, norm_mix_gain, norm_ffn_gain, ada_w, ada_b, pool_w, pool_scale, hgrn_w_in,
           hgrn_lb_logits, hgrn_norm_gain, hgrn_w_out, ffn_w_in, ffn_w_out, final_gain):
    depth = ada_w.shape[0]
    bsz, _, d = x.shape
    n_mixers = 2
    mod = _ada_mod(c, ada_w, ada_b).reshape(depth, bsz, 6, d)
    fgain = final_gain.reshape(1, d)
    for i in range(depth):
        j = i // n_mixers
        gain = norm_mix_gain[i].reshape(1, d)
        if i % n_mixers == 0:
            x = _pool_layer(x, mod, i, gain, pool_w[j].astype(BF16), pool_scale[j].reshape(1, d))
        else:
            x = _hgrn_layer(x, mod, i, gain, hgrn_lb_logits, j, hgrn_w_in[j].astype(BF16),
                            hgrn_norm_gain[j].reshape(1, d), hgrn_w_out[j].astype(BF16))
        x = _ffn_layer(x, mod, i, norm_ffn_gain[i].reshape(1, d), ffn_w_in[i].astype(BF16),
                       ffn_w_out[i].astype(BF16), fgain, final=(i == depth - 1))
    return x
```

```python
import functools

import jax
import jax.numpy as jnp
from jax import lax
from jax.experimental import pallas as pl
from jax.experimental.pallas import tpu as pltpu

EPS = 1e-6
POOL_WINDOWS = (2, 4, 8, 16)
POOL_HALO = 16
POOL_BLOCK = 128
POOL_BAND = 256
HEAD_DIM = 128
CHUNK = 32
SUPER = 128
MAX_SPAN = 64.0
HGRN_STREAMS = 2
VMEM_LIMIT = 56 * 1024 * 1024

F32 = jnp.float32
BF16 = jnp.bfloat16


def _dot(a, b):
    return jnp.dot(a, b, preferred_element_type=F32)


def _dot_nt(a, b):
    return lax.dot_general(a, b, (((1,), (1,)), ((), ())), preferred_element_type=F32)


def _dot_tn(a, b):
    return lax.dot_general(a, b, (((0,), (0,)), ((), ())), preferred_element_type=F32)


def _const_spec(shape):
    nd = len(shape)
    return pl.BlockSpec(shape, lambda *_: (0,) * nd)


def _norm_mod(x, gain, shift, scale):
    ms = jnp.mean(x * x, axis=-1, keepdims=True)
    return (x * lax.rsqrt(ms + EPS)) * (gain * (1.0 + scale)) + shift


def _ada_kernel(c_ref, w_ref, b_ref, o_ref):
    c = c_ref[...]
    ca = (c * jax.nn.sigmoid(c)).astype(BF16)
    o_ref[...] = _dot(ca, w_ref[...].astype(BF16)) + b_ref[...]


def _ada_mod(c, ada_w, ada_b):
    depth, d, n = ada_w.shape
    bsz = c.shape[0]
    tn = n // 4
    return pl.pallas_call(
        _ada_kernel,
        grid=(depth, n // tn),
        in_specs=[
            pl.BlockSpec((bsz, d), lambda l, j: (0, 0)),
            pl.BlockSpec((None, d, tn), lambda l, j: (l, 0, j)),
            pl.BlockSpec((None, 1, tn), lambda l, j: (l, 0, j)),
        ],
        out_specs=pl.BlockSpec((None, bsz, tn), lambda l, j: (l, 0, j)),
        out_shape=jax.ShapeDtypeStruct((depth, bsz, n), F32),
        compiler_params=pltpu.CompilerParams(
            dimension_semantics=("arbitrary", "arbitrary"),
            vmem_limit_bytes=VMEM_LIMIT),
        name="ada_mod",
    )(c, ada_w, ada_b.reshape(depth, 1, n))


def _ffn_kernel(x_ref, mod_ref, gain_ref, win_ref, wout_ref, fgain_ref, o_ref, act_ref,
                *, fc, final):
    f = wout_ref.shape[0]
    m = mod_ref[...]
    h = _norm_mod(x_ref[...], gain_ref[...], m[3:4], m[4:5]).astype(BF16)
    for c in range(f // fc):
        a = _dot(h, win_ref[:, c * fc:(c + 1) * fc])
        b = _dot(h, win_ref[:, f + c * fc:f + (c + 1) * fc])
        act_ref[:, c * fc:(c + 1) * fc] = (a * jax.nn.sigmoid(a) * b).astype(BF16)
    y = _dot(act_ref[...], wout_ref[...])
    out = x_ref[...] + m[5:6] * y
    if final:
        ms = jnp.mean(out * out, axis=-1, keepdims=True)
        out = out * lax.rsqrt(ms + EPS) * fgain_ref[...]
    o_ref[...] = out


def _ffn_layer(x, mod, layer, gain, w_in, w_out, final_gain, *, final, tm=512, fc=256):
    bsz, t, d = x.shape
    f = w_out.shape[0]
    kern = functools.partial(_ffn_kernel, fc=fc, final=final)
    return pl.pallas_call(
        kern,
        grid=(bsz, t // tm),
        in_specs=[
            pl.BlockSpec((None, tm, d), lambda b, i: (b, i, 0)),
            pl.BlockSpec((None, None, 6, d), lambda b, i: (layer, b, 0, 0)),
            _const_spec((1, d)),
            _const_spec((d, 2 * f)),
            _const_spec((f, d)),
            _const_spec((1, d)),
        ],
        out_specs=pl.BlockSpec((None, tm, d), lambda b, i: (b, i, 0)),
        out_shape=jax.ShapeDtypeStruct(x.shape, F32),
        scratch_shapes=[pltpu.VMEM((tm, f), BF16)],
        compiler_params=pltpu.CompilerParams(
            dimension_semantics=("parallel", "parallel"),
            vmem_limit_bytes=VMEM_LIMIT),
        name="ffn",
    )(x, mod, gain, w_in, w_out, final_gain)


def _pool_kernel(x_ref, halo_ref, mod_ref, gain_ref, band_ref, pw_ref, ps_ref, o_ref,
                 hf_ref, hi_ref, lo_ref, d_ref):
    i = pl.program_id(1)
    tt, d = x_ref.shape
    gw = d // len(POOL_WINDOWS)
    m = mod_ref[...]
    gain = gain_ref[...]

    def split(a):
        hi = a.astype(BF16)
        return hi, (a - hi.astype(F32)).astype(BF16)

    h = _norm_mod(x_ref[...], gain, m[0:1], m[1:2])
    hf_ref[...] = h
    hh = jnp.where(i == 0, 0.0, _norm_mod(halo_ref[...], gain, m[0:1], m[1:2]))
    for ref, top, body in zip((hi_ref, lo_ref), split(hh), split(h)):
        ref[0:POOL_HALO, :] = top
        ref[POOL_HALO:POOL_HALO + tt, :] = body
        ref[POOL_HALO + tt:, :] = jnp.zeros((ref.shape[0] - POOL_HALO - tt, d), BF16)
    pos = i * tt + lax.broadcasted_iota(jnp.int32, (tt, 1), 0)
    for g, w in enumerate(POOL_WINDOWS):
        cols = slice(g * gw, (g + 1) * gw)
        inv_cnt = 1.0 / jnp.minimum(pos + 1, w).astype(F32)
        band = band_ref[g]
        for blk in range(tt // POOL_BLOCK):
            rows = slice(blk * POOL_BLOCK, (blk + 1) * POOL_BLOCK)
            win = slice(blk * POOL_BLOCK, blk * POOL_BLOCK + POOL_BAND)
            s = _dot(band, hi_ref[win, cols]) + _dot(band, lo_ref[win, cols])
            d_ref[rows, :] = (s * inv_cnt[rows] - hf_ref[rows, cols]).astype(BF16)
        y = _dot(d_ref[...], pw_ref[g]) * ps_ref[:, cols]
        o_ref[:, cols] = x_ref[:, cols] + m[2:3, cols] * y


def _pool_bands():
    i = jnp.arange(POOL_BLOCK)[:, None] + POOL_HALO
    j = jnp.arange(POOL_BAND)[None, :]
    return jnp.stack([((j <= i) & (j > i - w)) for w in POOL_WINDOWS]).astype(BF16)


def _pool_layer(x, mod, layer, gain, pool_w, pool_scale, *, tt=512):
    bsz, t, d = x.shape
    gw = d // len(POOL_WINDOWS)
    hb = tt // POOL_HALO
    ext = tt - POOL_BLOCK + POOL_BAND
    bands = _pool_bands()
    return pl.pallas_call(
        _pool_kernel,
        grid=(bsz, t // tt),
        in_specs=[
            pl.BlockSpec((None, tt, d), lambda b, i: (b, i, 0)),
            pl.BlockSpec((None, POOL_HALO, d), lambda b, i: (b, jnp.maximum(i * hb - 1, 0), 0)),
            pl.BlockSpec((None, None, 6, d), lambda b, i: (layer, b, 0, 0)),
            _const_spec((1, d)),
            _const_spec(bands.shape),
            _const_spec(pool_w.shape),
            _const_spec((1, d)),
        ],
        out_specs=pl.BlockSpec((None, tt, d), lambda b, i: (b, i, 0)),
        out_shape=jax.ShapeDtypeStruct(x.shape, F32),
        scratch_shapes=[
            pltpu.VMEM((tt, d), F32),
            pltpu.VMEM((ext, d), BF16),
            pltpu.VMEM((ext, d), BF16),
            pltpu.VMEM((tt, gw), BF16),
        ],
        compiler_params=pltpu.CompilerParams(
            dimension_semantics=("parallel", "parallel"),
            vmem_limit_bytes=VMEM_LIMIT),
        name="pool_mixer",
    )(x, x, mod, gain, bands, pool_w, pool_scale)


def _lower_bound(logits, row):
    mx = jnp.max(logits, axis=0, keepdims=True)
    e = jnp.exp(logits - mx)
    sm = e / jnp.sum(e, axis=0, keepdims=True)
    acc = sm[0:1]
    for r in range(1, row + 1):
        acc = acc + sm[r:r + 1]
    return acc - sm[0:1]


def _hgrn_kernel(x_ref, mod_ref, gain_ref, lbl_ref, tri_ref, win_ref, hgain_ref, wout_ref, o_ref,
                 st_ref, qt_ref, kt_ref, kh_ref, v_ref, b_ref, qk_ref, dl_ref, gate_ref,
                 oacc_ref, oin_ref, *, row):
    i = pl.program_id(1)
    ns, tt, d = x_ref.shape
    streams = range(ns)
    head_cols = [slice(hd * HEAD_DIM, (hd + 1) * HEAD_DIM) for hd in range(d // HEAD_DIM)]
    n_chunks = tt // CHUNK

    @pl.when(i == 0)
    def _():
        st_ref[...] = jnp.zeros_like(st_ref)

    gain = gain_ref[...]
    lb = _lower_bound(lbl_ref[...], row)
    tri = tri_ref[...]

    q, z = [], []
    for s in streams:
        m = mod_ref[s]
        h = _norm_mod(x_ref[s], gain, m[0:1], m[1:2]).astype(BF16)
        q.append(_dot(h, win_ref[:, 0:d]))
        z.append(_dot(h, win_ref[:, d:2 * d]))
        v = _dot(h, win_ref[:, 2 * d:3 * d])
        v_ref[s] = v.astype(BF16)
        qk_ref[s, 2] = v
        og = _dot(h, win_ref[:, 3 * d:4 * d])
        gate_ref[s] = og * jax.nn.sigmoid(og)

    k, g_hi, g_lo = [], [], []
    for s in streams:
        e = jnp.exp(-z[s])
        sg = 1.0 / (1.0 + e)
        logf = jnp.log(lb + (1.0 - lb) * sg)
        k.append((1.0 - lb) * (e * sg))
        hi = logf.astype(BF16)
        g_hi.append(hi)
        g_lo.append((logf - hi.astype(F32)).astype(BF16))

    b = [_dot(tri, g_hi[s]) + _dot(tri, g_lo[s]) for s in streams]

    span = jnp.zeros((1, d), F32)
    for s in streams:
        b_ref[s] = b[s]
        qk_ref[s, 0] = q[s]
        qk_ref[s, 1] = k[s]
        qt_ref[s] = (q[s] * jnp.exp(b[s])).astype(BF16)
        kt_ref[s] = (k[s] * jnp.exp(-b[s])).astype(BF16)
        for cidx in range(n_chunks):
            rows = slice(cidx * CHUNK, (cidx + 1) * CHUNK)
            bl = b[s][(cidx + 1) * CHUNK - 1:(cidx + 1) * CHUNK]
            kh_ref[s, rows, :] = (k[s][rows] * jnp.exp(bl - b[s][rows])).astype(BF16)
            dl_ref[s, cidx] = jnp.exp(bl)
            span = jnp.maximum(span, -bl)
    fast = jnp.max(span) <= MAX_SPAN

    ri = lax.broadcasted_iota(jnp.int32, (SUPER, SUPER), 0)
    ci = lax.broadcasted_iota(jnp.int32, (SUPER, SUPER), 1)
    causal = (ri // CHUNK == ci // CHUNK) & (ci <= ri)
    for sc in range(tt // SUPER):
        rows = slice(sc * SUPER, (sc + 1) * SUPER)
        for s in streams:
            atts = [_dot_nt(qt_ref[s, rows, cols], kt_ref[s, rows, cols]) for cols in head_cols]
            for cols, att in zip(head_cols, atts):
                att = jnp.where(causal, att, 0.0).astype(BF16)
                oin_ref[s, rows, cols] = _dot(att, v_ref[s, rows, cols])

    for cidx in range(n_chunks):
        rows = slice(cidx * CHUNK, (cidx + 1) * CHUNK)
        for s in streams:
            for hd, cols in enumerate(head_cols):
                oacc_ref[s, rows, cols] = _dot_nt(qt_ref[s, rows, cols], st_ref[s, hd].astype(BF16))
        for s in streams:
            for hd, cols in enumerate(head_cols):
                dst = _dot_tn(v_ref[s, rows, cols], kh_ref[s, rows, cols])
                st_ref[s, hd] = st_ref[s, hd] * dl_ref[s, cidx, :, cols] + dst

    @pl.when(jnp.logical_not(fast))
    def _():
        rowi = lax.broadcasted_iota(jnp.int32, (CHUNK, 1), 0)

        def body(cidx, carry):
            rows = pl.ds(pl.multiple_of(cidx * CHUNK, CHUNK), CHUNK)
            for s in streams:
                for cols in head_cols:
                    bb = b_ref[s, rows, cols]
                    qq = qk_ref[s, 0, rows, cols]
                    kk = qk_ref[s, 1, rows, cols]
                    vv = qk_ref[s, 2, rows, cols]
                    acc = jnp.zeros((CHUNK, HEAD_DIM), F32)
                    for j in range(CHUNK):
                        p = qq * kk[j:j + 1] * jnp.exp(jnp.minimum(bb - bb[j:j + 1], 0.0))
                        col = jnp.sum(p, axis=-1, keepdims=True)
                        acc = acc + jnp.where(rowi >= j, col, 0.0) * vv[j:j + 1]
                    oin_ref[s, rows, cols] = acc
            return carry
        lax.fori_loop(0, n_chunks, body, 0)

    hg = hgain_ref[...]
    for s in streams:
        for cols in head_cols:
            o = oacc_ref[s, :, cols] + oin_ref[s, :, cols]
            ms = jnp.mean(o * o, axis=-1, keepdims=True)
            oacc_ref[s, :, cols] = o * lax.rsqrt(ms + EPS) * hg[:, cols] * gate_ref[s, :, cols]
        y = _dot(oacc_ref[s].astype(BF16), wout_ref[...])
        o_ref[s] = x_ref[s] + mod_ref[s][2:3] * y


def _chunk_tri(tt):
    r = jnp.arange(tt)[:, None]
    c = jnp.arange(tt)[None, :]
    return ((r // CHUNK == c // CHUNK) & (c <= r)).astype(BF16)


def _hgrn_layer(x, mod, layer, gain, lb_logits, row, w_in, hgain, w_out, *, tt=256):
    bsz, t, d = x.shape
    ns = HGRN_STREAMS
    heads = d // HEAD_DIM
    n_chunks = tt // CHUNK
    kern = functools.partial(_hgrn_kernel, row=row)
    return pl.pallas_call(
        kern,
        grid=(bsz // ns, t // tt),
        in_specs=[
            pl.BlockSpec((ns, tt, d), lambda b, i: (b, i, 0)),
            pl.BlockSpec((None, ns, 6, d), lambda b, i: (layer, b, 0, 0)),
            _const_spec((1, d)),
            _const_spec(lb_logits.shape),
            _const_spec((tt, tt)),
            _const_spec((d, 4 * d)),
            _const_spec((1, d)),
            _const_spec((d, d)),
        ],
        out_specs=pl.BlockSpec((ns, tt, d), lambda b, i: (b, i, 0)),
        out_shape=jax.ShapeDtypeStruct(x.shape, F32),
        scratch_shapes=[
            pltpu.VMEM((ns, heads, HEAD_DIM, HEAD_DIM), F32),
            pltpu.VMEM((ns, tt, d), BF16),
            pltpu.VMEM((ns, tt, d), BF16),
            pltpu.VMEM((ns, tt, d), BF16),
            pltpu.VMEM((ns, tt, d), BF16),
            pltpu.VMEM((ns, tt, d), F32),
            pltpu.VMEM((ns, 3, tt, d), F32),
            pltpu.VMEM((ns, n_chunks, 1, d), F32),
            pltpu.VMEM((ns, tt, d), F32),
            pltpu.VMEM((ns, tt, d), F32),
            pltpu.VMEM((ns, tt, d), F32),
        ],
        compiler_params=pltpu.CompilerParams(
            dimension_semantics=("parallel", "arbitrary"),
            vmem_limit_bytes=VMEM_LIMIT),
        name="hgrn_mixer",
    )(x, mod, gain, lb_logits, _chunk_tri(tt), w_in, hgain, w_out)


def kernel(x, c, norm_mix_gain, norm_ffn_gain, ada_w, ada_b, pool_w, pool_scale, hgrn_w_in,
           hgrn_lb_logits, hgrn_norm_gain, hgrn_w_out, ffn_w_in, ffn_w_out, final_gain):
    depth = ada_w.shape[0]
    bsz, _, d = x.shape
    n_mixers = 2
    mod = _ada_mod(c, ada_w, ada_b).reshape(depth, bsz, 6, d)
    fgain = final_gain.reshape(1, d)
    for i in range(depth):
        j = i // n_mixers
        gain = norm_mix_gain[i].reshape(1, d)
        if i % n_mixers == 0:
            x = _pool_layer(x, mod, i, gain, pool_w[j].astype(BF16), pool_scale[j].reshape(1, d))
        else:
            x = _hgrn_layer(x, mod, i, gain, hgrn_lb_logits, j, hgrn_w_in[j].astype(BF16),
                            hgrn_norm_gain[j].reshape(1, d), hgrn_w_out[j].astype(BF16))
        x = _ffn_layer(x, mod, i, norm_ffn_gain[i].reshape(1, d), ffn_w_in[i].astype(BF16),
                       ffn_w_out[i].astype(BF16), fgain, final=(i == depth - 1))
    return x
```

```python
import functools

import jax
import jax.numpy as jnp
from jax import lax
from jax.experimental import pallas as pl
from jax.experimental.pallas import tpu as pltpu

EPS = 1e-6
POOL_WINDOWS = (2, 4, 8, 16)
POOL_HALO = 16
POOL_BLOCK = 128
POOL_BAND = 256
HEAD_DIM = 128
CHUNK = 32
SUPER = 128
MAX_SPAN = 64.0
HGRN_STREAMS = 2
VMEM_LIMIT = 56 * 1024 * 1024

F32 = jnp.float32
BF16 = jnp.bfloat16


def _dot(a, b):
    return jnp.dot(a, b, preferred_element_type=F32)


def _dot_nt(a, b):
    return lax.dot_general(a, b, (((1,), (1,)), ((), ())), preferred_element_type=F32)


def _dot_tn(a, b):
    return lax.dot_general(a, b, (((0,), (0,)), ((), ())), preferred_element_type=F32)


def _const_spec(shape):
    nd = len(shape)
    return pl.BlockSpec(shape, lambda *_: (0,) * nd)


def _layer_spec(arr, layer):
    nd = arr.ndim - 1
    return pl.BlockSpec((None,) + arr.shape[1:], lambda *_: (layer,) + (0,) * nd)


def _norm_mod(x, gain, shift, scale):
    ms = jnp.mean(x * x, axis=-1, keepdims=True)
    return (x * lax.rsqrt(ms + EPS)) * (gain * (1.0 + scale)) + shift


def _ada_kernel(c_ref, w_ref, b_ref, o_ref):
    c = c_ref[...]
    ca = (c * jax.nn.sigmoid(c)).astype(BF16)
    o_ref[...] = _dot(ca, w_ref[...].astype(BF16)) + b_ref[...]


def _ada_mod(c, ada_w, ada_b):
    depth, d, n = ada_w.shape
    bsz = c.shape[0]
    tn = n // 4
    return pl.pallas_call(
        _ada_kernel,
        grid=(depth, n // tn),
        in_specs=[
            pl.BlockSpec((bsz, d), lambda l, j: (0, 0)),
            pl.BlockSpec((None, d, tn), lambda l, j: (l, 0, j)),
            pl.BlockSpec((None, 1, tn), lambda l, j: (l, 0, j)),
        ],
        out_specs=pl.BlockSpec((None, bsz, tn), lambda l, j: (l, 0, j)),
        out_shape=jax.ShapeDtypeStruct((depth, bsz, n), F32),
        compiler_params=pltpu.CompilerParams(
            dimension_semantics=("arbitrary", "arbitrary"),
            vmem_limit_bytes=VMEM_LIMIT),
        name="ada_mod",
    )(c, ada_w, ada_b.reshape(depth, 1, n))


def _ffn_kernel(x_ref, mod_ref, gain_ref, win_ref, wout_ref, fgain_ref, o_ref, act_ref,
                *, fc, final):
    f = wout_ref.shape[0]
    m = mod_ref[...]
    h = _norm_mod(x_ref[...], gain_ref[...], m[3:4], m[4:5]).astype(BF16)
    for c in range(f // fc):
        a = _dot(h, win_ref[:, c * fc:(c + 1) * fc])
        b = _dot(h, win_ref[:, f + c * fc:f + (c + 1) * fc])
        act_ref[:, c * fc:(c + 1) * fc] = (a * jax.nn.sigmoid(a) * b).astype(BF16)
    y = _dot(act_ref[...], wout_ref[...])
    out = x_ref[...] + m[5:6] * y
    if final:
        ms = jnp.mean(out * out, axis=-1, keepdims=True)
        out = out * lax.rsqrt(ms + EPS) * fgain_ref[...]
    o_ref[...] = out


def _ffn_layer(x, mod, layer, gain, w_in, w_out, final_gain, *, final, tm=512, fc=256):
    bsz, t, d = x.shape
    f = w_out.shape[1]
    kern = functools.partial(_ffn_kernel, fc=fc, final=final)
    return pl.pallas_call(
        kern,
        grid=(bsz, t // tm),
        in_specs=[
            pl.BlockSpec((None, tm, d), lambda b, i: (b, i, 0)),
            pl.BlockSpec((None, None, 6, d), lambda b, i: (layer, b, 0, 0)),
            _layer_spec(gain, layer),
            _layer_spec(w_in, layer),
            _layer_spec(w_out, layer),
            _const_spec((1, d)),
        ],
        out_specs=pl.BlockSpec((None, tm, d), lambda b, i: (b, i, 0)),
        out_shape=jax.ShapeDtypeStruct(x.shape, F32),
        scratch_shapes=[pltpu.VMEM((tm, f), BF16)],
        compiler_params=pltpu.CompilerParams(
            dimension_semantics=("parallel", "parallel"),
            vmem_limit_bytes=VMEM_LIMIT),
        name="ffn",
    )(x, mod, gain, w_in, w_out, final_gain)


def _pool_kernel(x_ref, halo_ref, mod_ref, gain_ref, band_ref, pw_ref, ps_ref, o_ref,
                 hf_ref, hi_ref, lo_ref, d_ref):
    i = pl.program_id(1)
    tt, d = x_ref.shape
    gw = d // len(POOL_WINDOWS)
    m = mod_ref[...]
    gain = gain_ref[...]

    def split(a):
        hi = a.astype(BF16)
        return hi, (a - hi.astype(F32)).astype(BF16)

    h = _norm_mod(x_ref[...], gain, m[0:1], m[1:2])
    hf_ref[...] = h
    hh = jnp.where(i == 0, 0.0, _norm_mod(halo_ref[...], gain, m[0:1], m[1:2]))
    for ref, top, body in zip((hi_ref, lo_ref), split(hh), split(h)):
        ref[0:POOL_HALO, :] = top
        ref[POOL_HALO:POOL_HALO + tt, :] = body
        ref[POOL_HALO + tt:, :] = jnp.zeros((ref.shape[0] - POOL_HALO - tt, d), BF16)
    pos = i * tt + lax.broadcasted_iota(jnp.int32, (tt, 1), 0)
    for g, w in enumerate(POOL_WINDOWS):
        cols = slice(g * gw, (g + 1) * gw)
        inv_cnt = 1.0 / jnp.minimum(pos + 1, w).astype(F32)
        band = band_ref[g]
        for blk in range(tt // POOL_BLOCK):
            rows = slice(blk * POOL_BLOCK, (blk + 1) * POOL_BLOCK)
            win = slice(blk * POOL_BLOCK, blk * POOL_BLOCK + POOL_BAND)
            s = _dot(band, hi_ref[win, cols]) + _dot(band, lo_ref[win, cols])
            d_ref[rows, :] = (s * inv_cnt[rows] - hf_ref[rows, cols]).astype(BF16)
        y = _dot(d_ref[...], pw_ref[g]) * ps_ref[:, cols]
        o_ref[:, cols] = x_ref[:, cols] + m[2:3, cols] * y


def _pool_bands():
    i = jnp.arange(POOL_BLOCK)[:, None] + POOL_HALO
    j = jnp.arange(POOL_BAND)[None, :]
    return jnp.stack([((j <= i) & (j > i - w)) for w in POOL_WINDOWS]).astype(BF16)


def _pool_layer(x, mod, layer, gain, pool_w, pool_scale, row, *, tt=512):
    bsz, t, d = x.shape
    gw = d // len(POOL_WINDOWS)
    hb = tt // POOL_HALO
    ext = tt - POOL_BLOCK + POOL_BAND
    bands = _pool_bands()
    return pl.pallas_call(
        _pool_kernel,
        grid=(bsz, t // tt),
        in_specs=[
            pl.BlockSpec((None, tt, d), lambda b, i: (b, i, 0)),
            pl.BlockSpec((None, POOL_HALO, d), lambda b, i: (b, jnp.maximum(i * hb - 1, 0), 0)),
            pl.BlockSpec((None, None, 6, d), lambda b, i: (layer, b, 0, 0)),
            _layer_spec(gain, layer),
            _const_spec(bands.shape),
            _layer_spec(pool_w, row),
            _layer_spec(pool_scale, row),
        ],
        out_specs=pl.BlockSpec((None, tt, d), lambda b, i: (b, i, 0)),
        out_shape=jax.ShapeDtypeStruct(x.shape, F32),
        scratch_shapes=[
            pltpu.VMEM((tt, d), F32),
            pltpu.VMEM((ext, d), BF16),
            pltpu.VMEM((ext, d), BF16),
            pltpu.VMEM((tt, gw), BF16),
        ],
        compiler_params=pltpu.CompilerParams(
            dimension_semantics=("parallel", "parallel"),
            vmem_limit_bytes=VMEM_LIMIT),
        name="pool_mixer",
    )(x, x, mod, gain, bands, pool_w, pool_scale)


def _lower_bound(logits, row):
    mx = jnp.max(logits, axis=0, keepdims=True)
    e = jnp.exp(logits - mx)
    sm = e / jnp.sum(e, axis=0, keepdims=True)
    acc = sm[0:1]
    for r in range(1, row + 1):
        acc = acc + sm[r:r + 1]
    return acc - sm[0:1]


def _hgrn_kernel(x_ref, mod_ref, gain_ref, lbl_ref, tri_ref, win_ref, hgain_ref, wout_ref, o_ref,
                 st_ref, qt_ref, kt_ref, kh_ref, v_ref, b_ref, qk_ref, dl_ref, gate_ref,
                 oacc_ref, oin_ref, *, row):
    i = pl.program_id(1)
    ns, tt, d = x_ref.shape
    streams = range(ns)
    head_cols = [slice(hd * HEAD_DIM, (hd + 1) * HEAD_DIM) for hd in range(d // HEAD_DIM)]
    n_chunks = tt // CHUNK

    @pl.when(i == 0)
    def _():
        st_ref[...] = jnp.zeros_like(st_ref)
        dl_ref[...] = jnp.zeros_like(dl_ref)

    gain = gain_ref[...]
    lb = _lower_bound(lbl_ref[...], row)
    tri = tri_ref[...]

    h = jnp.concatenate(
        [_norm_mod(x_ref[s], gain, mod_ref[s][0:1], mod_ref[s][1:2]).astype(BF16) for s in streams])
    z = _dot(h, win_ref[:, d:2 * d])
    q = _dot(h, win_ref[:, 0:d])
    v = _dot(h, win_ref[:, 2 * d:3 * d])
    og = _dot(h, win_ref[:, 3 * d:4 * d])
    z, q, v, og = ([a[s * tt:(s + 1) * tt] for s in streams] for a in (z, q, v, og))
    for s in streams:
        v_ref[s] = v[s].astype(BF16)
        qk_ref[s, 2] = v[s]
        gate_ref[s] = og[s] * jax.nn.sigmoid(og[s])

    k, g_hi, g_lo = [], [], []
    for s in streams:
        e = jnp.exp(-z[s])
        sg = 1.0 / (1.0 + e)
        logf = jnp.log(lb + (1.0 - lb) * sg)
        k.append((1.0 - lb) * (e * sg))
        hi = logf.astype(BF16)
        g_hi.append(hi)
        g_lo.append((logf - hi.astype(F32)).astype(BF16))

    b = [_dot(tri, g_hi[s]) + _dot(tri, g_lo[s]) for s in streams]

    span = jnp.zeros((1, d), F32)
    for s in streams:
        b_ref[s] = b[s]
        qk_ref[s, 0] = q[s]
        qk_ref[s, 1] = k[s]
        qt_ref[s] = (q[s] * jnp.exp(b[s])).astype(BF16)
        kt_ref[s] = (k[s] * jnp.exp(-b[s])).astype(BF16)
        for cidx in range(n_chunks):
            rows = slice(cidx * CHUNK, (cidx + 1) * CHUNK)
            bl = b[s][(cidx + 1) * CHUNK - 1:(cidx + 1) * CHUNK]
            kh_ref[s, rows, :] = (k[s][rows] * jnp.exp(bl - b[s][rows])).astype(BF16)
            dl_ref[s, cidx:cidx + 1, :] = jnp.exp(bl)
            span = jnp.maximum(span, -bl)
    fast = jnp.max(span) <= MAX_SPAN

    ri = lax.broadcasted_iota(jnp.int32, (SUPER, SUPER), 0)
    ci = lax.broadcasted_iota(jnp.int32, (SUPER, SUPER), 1)
    causal = (ri // CHUNK == ci // CHUNK) & (ci <= ri)
    for sc in range(tt // SUPER):
        rows = slice(sc * SUPER, (sc + 1) * SUPER)
        for s in streams:
            atts = [_dot_nt(qt_ref[s, rows, cols], kt_ref[s, rows, cols]) for cols in head_cols]
            for cols, att in zip(head_cols, atts):
                att = jnp.where(causal, att, 0.0).astype(BF16)
                oin_ref[s, rows, cols] = _dot(att, v_ref[s, rows, cols])

    cpb = SUPER // CHUNK
    zero_rows = jnp.zeros((CHUNK, HEAD_DIM), BF16)
    dcols = {(s, hd): dl_ref[s, :, cols].T for s in streams for hd, cols in enumerate(head_cols)}
    for sc in range(tt // SUPER):
        rows = slice(sc * SUPER, (sc + 1) * SUPER)
        dst = {}
        for s in streams:
            for hd, cols in enumerate(head_cols):
                vv = v_ref[s, rows, cols]
                spread = jnp.concatenate(
                    [jnp.concatenate([vv[c * CHUNK:(c + 1) * CHUNK] if r == c else zero_rows
                                      for r in range(cpb)], axis=0) for c in range(cpb)], axis=1)
                dst[s, hd] = _dot_tn(kh_ref[s, rows, cols], spread)
        for c in range(cpb):
            cidx = sc * cpb + c
            crow = slice(cidx * CHUNK, (cidx + 1) * CHUNK)
            for s in streams:
                for hd, cols in enumerate(head_cols):
                    oacc_ref[s, crow, cols] = _dot(qt_ref[s, crow, cols], st_ref[s, hd].astype(BF16))
            for s in streams:
                for hd, cols in enumerate(head_cols):
                    upd = dst[s, hd][:, c * HEAD_DIM:(c + 1) * HEAD_DIM]
                    st_ref[s, hd] = st_ref[s, hd] * dcols[s, hd][:, cidx:cidx + 1] + upd

    @pl.when(jnp.logical_not(fast))
    def _():
        rowi = lax.broadcasted_iota(jnp.int32, (CHUNK, 1), 0)

        def body(cidx, carry):
            rows = pl.ds(pl.multiple_of(cidx * CHUNK, CHUNK), CHUNK)
            for s in streams:
                for cols in head_cols:
                    bb = b_ref[s, rows, cols]
                    qq = qk_ref[s, 0, rows, cols]
                    kk = qk_ref[s, 1, rows, cols]
                    vv = qk_ref[s, 2, rows, cols]
                    acc = jnp.zeros((CHUNK, HEAD_DIM), F32)
                    for j in range(CHUNK):
                        p = qq * kk[j:j + 1] * jnp.exp(jnp.minimum(bb - bb[j:j + 1], 0.0))
                        col = jnp.sum(p, axis=-1, keepdims=True)
                        acc = acc + jnp.where(rowi >= j, col, 0.0) * vv[j:j + 1]
                    oin_ref[s, rows, cols] = acc
            return carry
        lax.fori_loop(0, n_chunks, body, 0)

    hg = hgain_ref[...]
    for s in streams:
        for cols in head_cols:
            o = oacc_ref[s, :, cols] + oin_ref[s, :, cols]
            ms = jnp.mean(o * o, axis=-1, keepdims=True)
            oacc_ref[s, :, cols] = o * lax.rsqrt(ms + EPS) * hg[:, cols] * gate_ref[s, :, cols]
    y = _dot(oacc_ref[...].reshape(ns * tt, d).astype(BF16), wout_ref[...])
    for s in streams:
        o_ref[s] = x_ref[s] + mod_ref[s][2:3] * y[s * tt:(s + 1) * tt]


def _chunk_tri(tt):
    r = jnp.arange(tt)[:, None]
    c = jnp.arange(tt)[None, :]
    return ((r // CHUNK == c // CHUNK) & (c <= r)).astype(BF16)


def _hgrn_layer(x, mod, layer, gain, lb_logits, row, w_in, hgain, w_out, *, tt=256):
    bsz, t, d = x.shape
    ns = HGRN_STREAMS
    heads = d // HEAD_DIM
    n_chunks = tt // CHUNK
    kern = functools.partial(_hgrn_kernel, row=row)
    return pl.pallas_call(
        kern,
        grid=(bsz // ns, t // tt),
        in_specs=[
            pl.BlockSpec((ns, tt, d), lambda b, i: (b, i, 0)),
            pl.BlockSpec((None, ns, 6, d), lambda b, i: (layer, b, 0, 0)),
            _layer_spec(gain, layer),
            _const_spec(lb_logits.shape),
            _const_spec((tt, tt)),
            _layer_spec(w_in, row),
            _layer_spec(hgain, row),
            _layer_spec(w_out, row),
        ],
        out_specs=pl.BlockSpec((ns, tt, d), lambda b, i: (b, i, 0)),
        out_shape=jax.ShapeDtypeStruct(x.shape, F32),
        scratch_shapes=[
            pltpu.VMEM((ns, heads, HEAD_DIM, HEAD_DIM), F32),
            pltpu.VMEM((ns, tt, d), BF16),
            pltpu.VMEM((ns, tt, d), BF16),
            pltpu.VMEM((ns, tt, d), BF16),
            pltpu.VMEM((ns, tt, d), BF16),
            pltpu.VMEM((ns, tt, d), F32),
            pltpu.VMEM((ns, 3, tt, d), F32),
            pltpu.VMEM((ns, HEAD_DIM, d), F32),
            pltpu.VMEM((ns, tt, d), F32),
            pltpu.VMEM((ns, tt, d), F32),
            pltpu.VMEM((ns, tt, d), F32),
        ],
        compiler_params=pltpu.CompilerParams(
            dimension_semantics=("parallel", "arbitrary"),
            vmem_limit_bytes=VMEM_LIMIT),
        name="hgrn_mixer",
    )(x, mod, gain, lb_logits, _chunk_tri(tt), w_in, hgain, w_out)


def kernel(x, c, norm_mix_gain, norm_ffn_gain, ada_w, ada_b, pool_w, pool_scale, hgrn_w_in,
           hgrn_lb_logits, hgrn_norm_gain, hgrn_w_out, ffn_w_in, ffn_w_out, final_gain):
    depth = ada_w.shape[0]
    bsz, _, d = x.shape
    n_mixers = 2
    mod = _ada_mod(c, ada_w, ada_b).reshape(depth, bsz, 6, d)

    def rows(a):
        return a.reshape(a.shape[0], 1, d)

    mix_gain, ffn_gain = rows(norm_mix_gain), rows(norm_ffn_gain)
    pool_scale, hgrn_gain = rows(pool_scale), rows(hgrn_norm_gain)
    pool_w, hgrn_w_in, hgrn_w_out, ffn_w_in, ffn_w_out = (
        w.astype(BF16) for w in (pool_w, hgrn_w_in, hgrn_w_out, ffn_w_in, ffn_w_out))
    fgain = final_gain.reshape(1, d)
    for i in range(depth):
        j = i // n_mixers
        if i % n_mixers == 0:
            x = _pool_layer(x, mod, i, mix_gain, pool_w, pool_scale, j)
        else:
            x = _hgrn_layer(x, mod, i, mix_gain, hgrn_lb_logits, j, hgrn_w_in, hgrn_gain, hgrn_w_out)
        x = _ffn_layer(x, mod, i, ffn_gain, ffn_w_in, ffn_w_out, fgain, final=(i == depth - 1))
    return x
```

```python
import functools

import jax
import jax.numpy as jnp
from jax import lax
from jax.experimental import pallas as pl
from jax.experimental.pallas import tpu as pltpu

EPS = 1e-6
POOL_WINDOWS = (2, 4, 8, 16)
POOL_HALO = 16
POOL_BLOCK = 128
POOL_BAND = 256
HEAD_DIM = 128
CHUNK = 32
SUPER = 128
MAX_SPAN = 64.0
HGRN_STREAMS = 2
VMEM_LIMIT = 56 * 1024 * 1024

F32 = jnp.float32
BF16 = jnp.bfloat16


def _dot(a, b):
    return jnp.dot(a, b, preferred_element_type=F32)


def _dot_nt(a, b):
    return lax.dot_general(a, b, (((1,), (1,)), ((), ())), preferred_element_type=F32)


def _dot_tn(a, b):
    return lax.dot_general(a, b, (((0,), (0,)), ((), ())), preferred_element_type=F32)


def _const_spec(shape):
    nd = len(shape)
    return pl.BlockSpec(shape, lambda *_: (0,) * nd)


def _layer_spec(arr, layer):
    nd = arr.ndim - 1
    return pl.BlockSpec((None,) + arr.shape[1:], lambda *_: (layer,) + (0,) * nd)


def _norm_mod(x, gain, shift, scale):
    ms = jnp.mean(x * x, axis=-1, keepdims=True)
    return (x * lax.rsqrt(ms + EPS)) * (gain * (1.0 + scale)) + shift


def _ada_kernel(c_ref, w_ref, b_ref, o_ref):
    c = c_ref[...]
    ca = (c * jax.nn.sigmoid(c)).astype(BF16)
    o_ref[...] = _dot(ca, w_ref[...].astype(BF16)) + b_ref[...]


def _ada_mod(c, ada_w, ada_b):
    depth, d, n = ada_w.shape
    bsz = c.shape[0]
    tn = n // 4
    return pl.pallas_call(
        _ada_kernel,
        grid=(depth, n // tn),
        in_specs=[
            pl.BlockSpec((bsz, d), lambda l, j: (0, 0)),
            pl.BlockSpec((None, d, tn), lambda l, j: (l, 0, j)),
            pl.BlockSpec((None, 1, tn), lambda l, j: (l, 0, j)),
        ],
        out_specs=pl.BlockSpec((None, bsz, tn), lambda l, j: (l, 0, j)),
        out_shape=jax.ShapeDtypeStruct((depth, bsz, n), F32),
        compiler_params=pltpu.CompilerParams(
            dimension_semantics=("arbitrary", "arbitrary"),
            vmem_limit_bytes=VMEM_LIMIT),
        name="ada_mod",
    )(c, ada_w, ada_b.reshape(depth, 1, n))


def _pool_bands():
    i = jnp.arange(POOL_BLOCK)[:, None] + (POOL_BAND - POOL_BLOCK)
    j = jnp.arange(POOL_BAND)[None, :]
    return jnp.stack([((j <= i) & (j > i - w)) for w in POOL_WINDOWS]).astype(BF16)


def _split_bf16(a):
    hi = a.astype(BF16)
    return hi, (a - hi.astype(F32)).astype(BF16)


def _ffn_kernel(*refs, fc, final, pool, n_sub):
    if pool:
        (x_ref, halo_ref, mod_ref, mgain_ref, band_ref, pw_ref, ps_ref, gain_ref, win_ref,
         wout_ref, fgain_ref, o_ref, act_ref, x1_ref, hf_ref, hi_ref, lo_ref, d_ref) = refs
    else:
        x_ref, mod_ref, gain_ref, win_ref, wout_ref, fgain_ref, o_ref, act_ref = refs
    i = pl.program_id(1)
    tm, d = x_ref.shape
    f = wout_ref.shape[0]
    ts = tm // n_sub
    m = mod_ref[...]
    pad = POOL_BAND - POOL_BLOCK

    if pool:
        gw = d // len(POOL_WINDOWS)
        hh = jnp.where(i == 0, 0.0, _norm_mod(halo_ref[...], mgain_ref[...], m[0:1], m[1:2]))
        for ref, top in zip((hi_ref, lo_ref), _split_bf16(hh)):
            ref[0:pad - POOL_HALO, :] = jnp.zeros((pad - POOL_HALO, d), BF16)
            ref[pad - POOL_HALO:pad, :] = top

    def pool_split(rows):
        h = _norm_mod(x_ref[rows, :], mgain_ref[...], m[0:1], m[1:2])
        hf_ref[rows, :] = h
        for ref, piece in zip((hi_ref, lo_ref), _split_bf16(h)):
            ref[pad + rows.start:pad + rows.stop, :] = piece

    def pool_group(rows, g):
        w = POOL_WINDOWS[g]
        cols = slice(g * gw, (g + 1) * gw)
        n = rows.stop - rows.start
        pos = i * tm + rows.start + lax.broadcasted_iota(jnp.int32, (n, 1), 0)
        inv_cnt = 1.0 / jnp.minimum(pos + 1, w).astype(F32)
        for blk in range(n // POOL_BLOCK):
            r0 = rows.start + blk * POOL_BLOCK
            win = slice(r0, r0 + POOL_BAND)
            out = slice(r0, r0 + POOL_BLOCK)
            sm = _dot(band_ref[g], hi_ref[win, cols]) + _dot(band_ref[g], lo_ref[win, cols])
            loc = slice(blk * POOL_BLOCK, (blk + 1) * POOL_BLOCK)
            d_ref[out, cols] = (sm * inv_cnt[loc] - hf_ref[out, cols]).astype(BF16)
        y = _dot(d_ref[rows, cols], pw_ref[g]) * ps_ref[:, cols]
        x1_ref[rows, cols] = x_ref[rows, cols] + m[2:3, cols] * y

    src = x1_ref if pool else x_ref
    hs = []
    for j in range(n_sub):
        rows = slice(j * ts, (j + 1) * ts)
        if pool:
            pool_split(rows)
            for g in range(len(POOL_WINDOWS)):
                pool_group(rows, g)
        hs.append(_norm_mod(src[rows, :], gain_ref[...], m[3:4], m[4:5]).astype(BF16))
    for j in range(n_sub):
        rows = slice(j * ts, (j + 1) * ts)
        for c in range(f // fc):
            a = _dot(hs[j], win_ref[:, c * fc:(c + 1) * fc])
            b = _dot(hs[j], win_ref[:, f + c * fc:f + (c + 1) * fc])
            act_ref[rows, c * fc:(c + 1) * fc] = (a * jax.nn.sigmoid(a) * b).astype(BF16)
        y = _dot(act_ref[rows, :], wout_ref[...])
        out = src[rows, :] + m[5:6] * y
        if final:
            ms = jnp.mean(out * out, axis=-1, keepdims=True)
            out = out * lax.rsqrt(ms + EPS) * fgain_ref[...]
        o_ref[rows, :] = out


def _ffn_layer(x, mod, layer, gain, w_in, w_out, final_gain, *, final, pool=None,
               tm=512, fc=256, n_sub=2):
    bsz, t, d = x.shape
    f = w_out.shape[1]
    kern = functools.partial(_ffn_kernel, fc=fc, final=final, pool=pool is not None, n_sub=n_sub)
    x_spec = pl.BlockSpec((None, tm, d), lambda b, i: (b, i, 0))
    mod_spec = pl.BlockSpec((None, None, 6, d), lambda b, i: (layer, b, 0, 0))
    ffn_specs = [_layer_spec(gain, layer), _layer_spec(w_in, layer), _layer_spec(w_out, layer),
                 _const_spec((1, d))]
    ffn_args = [gain, w_in, w_out, final_gain]
    scratch = [pltpu.VMEM((tm, f), BF16)]
    if pool is None:
        in_specs = [x_spec, mod_spec] + ffn_specs
        args = [x, mod] + ffn_args
    else:
        mix_gain, pool_w, pool_scale, row = pool
        hb = tm // POOL_HALO
        bands = _pool_bands()
        halo_spec = pl.BlockSpec((None, POOL_HALO, d),
                                 lambda b, i: (b, jnp.maximum(i * hb - 1, 0), 0))
        in_specs = [x_spec, halo_spec, mod_spec, _layer_spec(mix_gain, layer),
                    _const_spec(bands.shape), _layer_spec(pool_w, row),
                    _layer_spec(pool_scale, row)] + ffn_specs
        args = [x, x, mod, mix_gain, bands, pool_w, pool_scale] + ffn_args
        ext = tm + POOL_BAND - POOL_BLOCK
        scratch += [
            pltpu.VMEM((tm, d), F32),
            pltpu.VMEM((tm, d), F32),
            pltpu.VMEM((ext, d), BF16),
            pltpu.VMEM((ext, d), BF16),
            pltpu.VMEM((tm, d), BF16),
        ]
    return pl.pallas_call(
        kern,
        grid=(bsz, t // tm),
        in_specs=in_specs,
        out_specs=pl.BlockSpec((None, tm, d), lambda b, i: (b, i, 0)),
        out_shape=jax.ShapeDtypeStruct(x.shape, F32),
        scratch_shapes=scratch,
        compiler_params=pltpu.CompilerParams(
            dimension_semantics=("parallel", "parallel"),
            vmem_limit_bytes=VMEM_LIMIT),
        name="pool_ffn" if pool is not None else "ffn",
    )(*args)


def _lower_bound(logits, row):
    mx = jnp.max(logits, axis=0, keepdims=True)
    e = jnp.exp(logits - mx)
    sm = e / jnp.sum(e, axis=0, keepdims=True)
    acc = sm[0:1]
    for r in range(1, row + 1):
        acc = acc + sm[r:r + 1]
    return acc - sm[0:1]


def _hgrn_kernel(x_ref, mod_ref, gain_ref, lbl_ref, tri_ref, win_ref, hgain_ref, wout_ref, o_ref,
                 st_ref, qt_ref, kt_ref, kh_ref, v_ref, b_ref, qk_ref, dl_ref, gate_ref,
                 oacc_ref, oin_ref, *, row):
    i = pl.program_id(1)
    ns, tt, d = x_ref.shape
    streams = range(ns)
    head_cols = [slice(hd * HEAD_DIM, (hd + 1) * HEAD_DIM) for hd in range(d // HEAD_DIM)]
    n_chunks = tt // CHUNK

    @pl.when(i == 0)
    def _():
        st_ref[...] = jnp.zeros_like(st_ref)
        dl_ref[...] = jnp.zeros_like(dl_ref)

    gain = gain_ref[...]
    lb = _lower_bound(lbl_ref[...], row)
    tri = tri_ref[...]

    h = jnp.concatenate(
        [_norm_mod(x_ref[s], gain, mod_ref[s][0:1], mod_ref[s][1:2]).astype(BF16) for s in streams])
    z = _dot(h, win_ref[:, d:2 * d])
    q = _dot(h, win_ref[:, 0:d])
    v = _dot(h, win_ref[:, 2 * d:3 * d])
    og = _dot(h, win_ref[:, 3 * d:4 * d])
    z, q, v, og = ([a[s * tt:(s + 1) * tt] for s in streams] for a in (z, q, v, og))
    for s in streams:
        v_ref[s] = v[s].astype(BF16)
        qk_ref[s, 2] = v[s]
        gate_ref[s] = og[s] * jax.nn.sigmoid(og[s])

    k, g_hi, g_lo = [], [], []
    for s in streams:
        e = jnp.exp(-z[s])
        sg = 1.0 / (1.0 + e)
        logf = jnp.log(lb + (1.0 - lb) * sg)
        k.append((1.0 - lb) * (e * sg))
        hi = logf.astype(BF16)
        g_hi.append(hi)
        g_lo.append((logf - hi.astype(F32)).astype(BF16))

    b = [_dot(tri, g_hi[s]) + _dot(tri, g_lo[s]) for s in streams]

    span = jnp.zeros((1, d), F32)
    for s in streams:
        b_ref[s] = b[s]
        qk_ref[s, 0] = q[s]
        qk_ref[s, 1] = k[s]
        qt_ref[s] = (q[s] * jnp.exp(b[s])).astype(BF16)
        kt_ref[s] = (k[s] * jnp.exp(-b[s])).astype(BF16)
        for cidx in range(n_chunks):
            rows = slice(cidx * CHUNK, (cidx + 1) * CHUNK)
            bl = b[s][(cidx + 1) * CHUNK - 1:(cidx + 1) * CHUNK]
            kh_ref[s, rows, :] = (k[s][rows] * jnp.exp(bl - b[s][rows])).astype(BF16)
            dl_ref[s, cidx:cidx + 1, :] = jnp.exp(bl)
            span = jnp.maximum(span, -bl)
    fast = jnp.max(span) <= MAX_SPAN

    ri = lax.broadcasted_iota(jnp.int32, (SUPER, SUPER), 0)
    ci = lax.broadcasted_iota(jnp.int32, (SUPER, SUPER), 1)
    causal = (ri // CHUNK == ci // CHUNK) & (ci <= ri)
    for sc in range(tt // SUPER):
        rows = slice(sc * SUPER, (sc + 1) * SUPER)
        for s in streams:
            atts = [_dot_nt(qt_ref[s, rows, cols], kt_ref[s, rows, cols]) for cols in head_cols]
            for cols, att in zip(head_cols, atts):
                att = jnp.where(causal, att, 0.0).astype(BF16)
                oin_ref[s, rows, cols] = _dot(att, v_ref[s, rows, cols])

    cpb = SUPER // CHUNK
    zero_rows = jnp.zeros((CHUNK, HEAD_DIM), BF16)
    dcols = {(s, hd): dl_ref[s, :, cols].T for s in streams for hd, cols in enumerate(head_cols)}
    for sc in range(tt // SUPER):
        rows = slice(sc * SUPER, (sc + 1) * SUPER)
        dst = {}
        for s in streams:
            for hd, cols in enumerate(head_cols):
                vv = v_ref[s, rows, cols]
                spread = jnp.concatenate(
                    [jnp.concatenate([vv[c * CHUNK:(c + 1) * CHUNK] if r == c else zero_rows
                                      for r in range(cpb)], axis=0) for c in range(cpb)], axis=1)
                dst[s, hd] = _dot_tn(kh_ref[s, rows, cols], spread)
        for c in range(cpb):
            cidx = sc * cpb + c
            crow = slice(cidx * CHUNK, (cidx + 1) * CHUNK)
            for s in streams:
                for hd, cols in enumerate(head_cols):
                    oacc_ref[s, crow, cols] = _dot(qt_ref[s, crow, cols], st_ref[s, hd].astype(BF16))
            for s in streams:
                for hd, cols in enumerate(head_cols):
                    upd = dst[s, hd][:, c * HEAD_DIM:(c + 1) * HEAD_DIM]
                    st_ref[s, hd] = st_ref[s, hd] * dcols[s, hd][:, cidx:cidx + 1] + upd

    @pl.when(jnp.logical_not(fast))
    def _():
        rowi = lax.broadcasted_iota(jnp.int32, (CHUNK, 1), 0)

        def body(cidx, carry):
            rows = pl.ds(pl.multiple_of(cidx * CHUNK, CHUNK), CHUNK)
            for s in streams:
                for cols in head_cols:
                    bb = b_ref[s, rows, cols]
                    qq = qk_ref[s, 0, rows, cols]
                    kk = qk_ref[s, 1, rows, cols]
                    vv = qk_ref[s, 2, rows, cols]
                    acc = jnp.zeros((CHUNK, HEAD_DIM), F32)
                    for j in range(CHUNK):
                        p = qq * kk[j:j + 1] * jnp.exp(jnp.minimum(bb - bb[j:j + 1], 0.0))
                        col = jnp.sum(p, axis=-1, keepdims=True)
                        acc = acc + jnp.where(rowi >= j, col, 0.0) * vv[j:j + 1]
                    oin_ref[s, rows, cols] = acc
            return carry
        lax.fori_loop(0, n_chunks, body, 0)

    hg = hgain_ref[...]
    for s in streams:
        for cols in head_cols:
            o = oacc_ref[s, :, cols] + oin_ref[s, :, cols]
            ms = jnp.mean(o * o, axis=-1, keepdims=True)
            oacc_ref[s, :, cols] = o * lax.rsqrt(ms + EPS) * hg[:, cols] * gate_ref[s, :, cols]
    y = _dot(oacc_ref[...].reshape(ns * tt, d).astype(BF16), wout_ref[...])
    for s in streams:
        o_ref[s] = x_ref[s] + mod_ref[s][2:3] * y[s * tt:(s + 1) * tt]


def _chunk_tri(tt):
    r = jnp.arange(tt)[:, None]
    c = jnp.arange(tt)[None, :]
    return ((r // CHUNK == c // CHUNK) & (c <= r)).astype(BF16)


def _hgrn_layer(x, mod, layer, gain, lb_logits, row, w_in, hgain, w_out, *, tt=256):
    bsz, t, d = x.shape
    ns = HGRN_STREAMS
    heads = d // HEAD_DIM
    n_chunks = tt // CHUNK
    kern = functools.partial(_hgrn_kernel, row=row)
    return pl.pallas_call(
        kern,
        grid=(bsz // ns, t // tt),
        in_specs=[
            pl.BlockSpec((ns, tt, d), lambda b, i: (b, i, 0)),
            pl.BlockSpec((None, ns, 6, d), lambda b, i: (layer, b, 0, 0)),
            _layer_spec(gain, layer),
            _const_spec(lb_logits.shape),
            _const_spec((tt, tt)),
            _layer_spec(w_in, row),
            _layer_spec(hgain, row),
            _layer_spec(w_out, row),
        ],
        out_specs=pl.BlockSpec((ns, tt, d), lambda b, i: (b, i, 0)),
        out_shape=jax.ShapeDtypeStruct(x.shape, F32),
        scratch_shapes=[
            pltpu.VMEM((ns, heads, HEAD_DIM, HEAD_DIM), F32),
            pltpu.VMEM((ns, tt, d), BF16),
            pltpu.VMEM((ns, tt, d), BF16),
            pltpu.VMEM((ns, tt, d), BF16),
            pltpu.VMEM((ns, tt, d), BF16),
            pltpu.VMEM((ns, tt, d), F32),
            pltpu.VMEM((ns, 3, tt, d), F32),
            pltpu.VMEM((ns, HEAD_DIM, d), F32),
            pltpu.VMEM((ns, tt, d), F32),
            pltpu.VMEM((ns, tt, d), F32),
            pltpu.VMEM((ns, tt, d), F32),
        ],
        compiler_params=pltpu.CompilerParams(
            dimension_semantics=("parallel", "arbitrary"),
            vmem_limit_bytes=VMEM_LIMIT),
        name="hgrn_mixer",
    )(x, mod, gain, lb_logits, _chunk_tri(tt), w_in, hgain, w_out)


def kernel(x, c, norm_mix_gain, norm_ffn_gain, ada_w, ada_b, pool_w, pool_scale, hgrn_w_in,
           hgrn_lb_logits, hgrn_norm_gain, hgrn_w_out, ffn_w_in, ffn_w_out, final_gain):
    depth = ada_w.shape[0]
    bsz, _, d = x.shape
    n_mixers = 2
    mod = _ada_mod(c, ada_w, ada_b).reshape(depth, bsz, 6, d)

    def rows(a):
        return a.reshape(a.shape[0], 1, d)

    mix_gain, ffn_gain = rows(norm_mix_gain), rows(norm_ffn_gain)
    pool_scale, hgrn_gain = rows(pool_scale), rows(hgrn_norm_gain)
    pool_w, hgrn_w_in, hgrn_w_out, ffn_w_in, ffn_w_out = (
        w.astype(BF16) for w in (pool_w, hgrn_w_in, hgrn_w_out, ffn_w_in, ffn_w_out))
    fgain = final_gain.reshape(1, d)
    for i in range(depth):
        j = i // n_mixers
        pool = None
        if i % n_mixers == 0:
            pool = (mix_gain, pool_w, pool_scale, j)
        else:
            x = _hgrn_layer(x, mod, i, mix_gain, hgrn_lb_logits, j, hgrn_w_in, hgrn_gain, hgrn_w_out)
        x = _ffn_layer(x, mod, i, ffn_gain, ffn_w_in, ffn_w_out, fgain, final=(i == depth - 1),
                       pool=pool)
    return x
```

```python
import functools

import jax
import jax.numpy as jnp
from jax import lax
from jax.experimental import pallas as pl
from jax.experimental.pallas import tpu as pltpu

EPS = 1e-6
POOL_WINDOWS = (2, 4, 8, 16)
POOL_HALO = 16
POOL_BLOCK = 128
POOL_BAND = 256
HEAD_DIM = 128
CHUNK = 32
SUPER = 128
MAX_SPAN = 64.0
HGRN_STREAMS = 2
VMEM_LIMIT = 56 * 1024 * 1024

F32 = jnp.float32
BF16 = jnp.bfloat16


def _dot(a, b):
    return jnp.dot(a, b, preferred_element_type=F32)


def _dot_nt(a, b):
    return lax.dot_general(a, b, (((1,), (1,)), ((), ())), preferred_element_type=F32)


def _dot_tn(a, b):
    return lax.dot_general(a, b, (((0,), (0,)), ((), ())), preferred_element_type=F32)


def _const_spec(shape):
    nd = len(shape)
    return pl.BlockSpec(shape, lambda *_: (0,) * nd)


def _layer_spec(arr, layer):
    nd = arr.ndim - 1
    return pl.BlockSpec((None,) + arr.shape[1:], lambda *_: (layer,) + (0,) * nd)


def _norm_mod(x, gain, shift, scale):
    ms = jnp.mean(x * x, axis=-1, keepdims=True)
    return (x * lax.rsqrt(ms + EPS)) * (gain * (1.0 + scale)) + shift


def _ada_kernel(c_ref, w_ref, b_ref, o_ref):
    c = c_ref[...]
    ca = (c * jax.nn.sigmoid(c)).astype(BF16)
    o_ref[...] = _dot(ca, w_ref[...].astype(BF16)) + b_ref[...]


def _ada_mod(c, ada_w, ada_b):
    depth, d, n = ada_w.shape
    bsz = c.shape[0]
    tn = n // 4
    return pl.pallas_call(
        _ada_kernel,
        grid=(depth, n // tn),
        in_specs=[
            pl.BlockSpec((bsz, d), lambda l, j: (0, 0)),
            pl.BlockSpec((None, d, tn), lambda l, j: (l, 0, j)),
            pl.BlockSpec((None, 1, tn), lambda l, j: (l, 0, j)),
        ],
        out_specs=pl.BlockSpec((None, bsz, tn), lambda l, j: (l, 0, j)),
        out_shape=jax.ShapeDtypeStruct((depth, bsz, n), F32),
        compiler_params=pltpu.CompilerParams(
            dimension_semantics=("arbitrary", "arbitrary"),
            vmem_limit_bytes=VMEM_LIMIT),
        name="ada_mod",
    )(c, ada_w, ada_b.reshape(depth, 1, n))


def _pool_bands():
    i = jnp.arange(POOL_BLOCK)[:, None] + (POOL_BAND - POOL_BLOCK)
    j = jnp.arange(POOL_BAND)[None, :]
    return jnp.stack([((j <= i) & (j > i - w)) for w in POOL_WINDOWS]).astype(BF16)


def _split_bf16(a):
    hi = a.astype(BF16)
    return hi, (a - hi.astype(F32)).astype(BF16)


def _ffn_kernel(*refs, fc, final, pool, n_sub):
    if pool:
        (x_ref, halo_ref, mod_ref, mgain_ref, band_ref, pw_ref, ps_ref, gain_ref, win_ref,
         wout_ref, fgain_ref, o_ref, act_ref, x1_ref, hf_ref, hi_ref, lo_ref, d_ref) = refs
    else:
        x_ref, mod_ref, gain_ref, win_ref, wout_ref, fgain_ref, o_ref, act_ref = refs
    i = pl.program_id(1)
    tm, d = x_ref.shape
    f = wout_ref.shape[0]
    ts = tm // n_sub
    m = mod_ref[...]
    pad = POOL_BAND - POOL_BLOCK

    if pool:
        gw = d // len(POOL_WINDOWS)
        hh = jnp.where(i == 0, 0.0, _norm_mod(halo_ref[...], mgain_ref[...], m[0:1], m[1:2]))
        for ref, top in zip((hi_ref, lo_ref), _split_bf16(hh)):
            ref[0:pad - POOL_HALO, :] = jnp.zeros((pad - POOL_HALO, d), BF16)
            ref[pad - POOL_HALO:pad, :] = top

    def pool_split(rows):
        h = _norm_mod(x_ref[rows, :], mgain_ref[...], m[0:1], m[1:2])
        hf_ref[rows, :] = h
        for ref, piece in zip((hi_ref, lo_ref), _split_bf16(h)):
            ref[pad + rows.start:pad + rows.stop, :] = piece

    def pool_group(rows, g):
        w = POOL_WINDOWS[g]
        cols = slice(g * gw, (g + 1) * gw)
        n = rows.stop - rows.start
        pos = i * tm + rows.start + lax.broadcasted_iota(jnp.int32, (n, 1), 0)
        inv_cnt = 1.0 / jnp.minimum(pos + 1, w).astype(F32)
        for blk in range(n // POOL_BLOCK):
            r0 = rows.start + blk * POOL_BLOCK
            win = slice(r0, r0 + POOL_BAND)
            out = slice(r0, r0 + POOL_BLOCK)
            sm = _dot(band_ref[g], hi_ref[win, cols]) + _dot(band_ref[g], lo_ref[win, cols])
            loc = slice(blk * POOL_BLOCK, (blk + 1) * POOL_BLOCK)
            d_ref[out, cols] = (sm * inv_cnt[loc] - hf_ref[out, cols]).astype(BF16)
        y = _dot(d_ref[rows, cols], pw_ref[g]) * ps_ref[:, cols]
        x1_ref[rows, cols] = x_ref[rows, cols] + m[2:3, cols] * y

    src = x1_ref if pool else x_ref
    hs = []
    for j in range(n_sub):
        rows = slice(j * ts, (j + 1) * ts)
        if pool:
            pool_split(rows)
            for g in range(len(POOL_WINDOWS)):
                pool_group(rows, g)
        hs.append(_norm_mod(src[rows, :], gain_ref[...], m[3:4], m[4:5]).astype(BF16))
    for j in range(n_sub):
        rows = slice(j * ts, (j + 1) * ts)
        for c in range(f // fc):
            a = _dot(hs[j], win_ref[:, c * fc:(c + 1) * fc])
            b = _dot(hs[j], win_ref[:, f + c * fc:f + (c + 1) * fc])
            act_ref[rows, c * fc:(c + 1) * fc] = (a * jax.nn.sigmoid(a) * b).astype(BF16)
        y = _dot(act_ref[rows, :], wout_ref[...])
        out = src[rows, :] + m[5:6] * y
        if final:
            ms = jnp.mean(out * out, axis=-1, keepdims=True)
            out = out * lax.rsqrt(ms + EPS) * fgain_ref[...]
        o_ref[rows, :] = out


def _ffn_layer(x, mod, layer, gain, w_in, w_out, final_gain, *, final, pool=None,
               tm=512, fc=256, n_sub=2):
    bsz, t, d = x.shape
    f = w_out.shape[1]
    kern = functools.partial(_ffn_kernel, fc=fc, final=final, pool=pool is not None, n_sub=n_sub)
    x_spec = pl.BlockSpec((None, tm, d), lambda b, i: (b, i, 0))
    mod_spec = pl.BlockSpec((None, None, 6, d), lambda b, i: (layer, b, 0, 0))
    ffn_specs = [_layer_spec(gain, layer), _layer_spec(w_in, layer), _layer_spec(w_out, layer),
                 _const_spec((1, d))]
    ffn_args = [gain, w_in, w_out, final_gain]
    scratch = [pltpu.VMEM((tm, f), BF16)]
    if pool is None:
        in_specs = [x_spec, mod_spec] + ffn_specs
        args = [x, mod] + ffn_args
    else:
        mix_gain, pool_w, pool_scale, row = pool
        hb = tm // POOL_HALO
        bands = _pool_bands()
        halo_spec = pl.BlockSpec((None, POOL_HALO, d),
                                 lambda b, i: (b, jnp.maximum(i * hb - 1, 0), 0))
        in_specs = [x_spec, halo_spec, mod_spec, _layer_spec(mix_gain, layer),
                    _const_spec(bands.shape), _layer_spec(pool_w, row),
                    _layer_spec(pool_scale, row)] + ffn_specs
        args = [x, x, mod, mix_gain, bands, pool_w, pool_scale] + ffn_args
        ext = tm + POOL_BAND - POOL_BLOCK
        scratch += [
            pltpu.VMEM((tm, d), F32),
            pltpu.VMEM((tm, d), F32),
            pltpu.VMEM((ext, d), BF16),
            pltpu.VMEM((ext, d), BF16),
            pltpu.VMEM((tm, d), BF16),
        ]
    return pl.pallas_call(
        kern,
        grid=(bsz, t // tm),
        in_specs=in_specs,
        out_specs=pl.BlockSpec((None, tm, d), lambda b, i: (b, i, 0)),
        out_shape=jax.ShapeDtypeStruct(x.shape, F32),
        scratch_shapes=scratch,
        compiler_params=pltpu.CompilerParams(
            dimension_semantics=("parallel", "parallel"),
            vmem_limit_bytes=VMEM_LIMIT),
        name="pool_ffn" if pool is not None else "ffn",
    )(*args)


def _lower_bound(logits, row):
    mx = jnp.max(logits, axis=0, keepdims=True)
    e = jnp.exp(logits - mx)
    sm = e / jnp.sum(e, axis=0, keepdims=True)
    acc = sm[0:1]
    for r in range(1, row + 1):
        acc = acc + sm[r:r + 1]
    return acc - sm[0:1]


def _hgrn_kernel(x_ref, mod_ref, gain_ref, lbl_ref, tri_ref, win_ref, hgain_ref, wout_ref, o_ref,
                 st_ref, qt_ref, kt_ref, kh_ref, v_ref, b_ref, qk_ref, dl_ref, gate_ref,
                 oacc_ref, oin_ref, *, row):
    i = pl.program_id(1)
    ns, tt, d = x_ref.shape
    streams = range(ns)
    head_cols = [slice(hd * HEAD_DIM, (hd + 1) * HEAD_DIM) for hd in range(d // HEAD_DIM)]
    n_chunks = tt // CHUNK

    @pl.when(i == 0)
    def _():
        st_ref[...] = jnp.zeros_like(st_ref)
        dl_ref[...] = jnp.zeros_like(dl_ref)

    gain = gain_ref[...]
    lb = _lower_bound(lbl_ref[...], row)
    tri = tri_ref[...]

    def per_stream(a):
        return [a[s * tt:(s + 1) * tt] for s in streams]

    h = jnp.concatenate(
        [_norm_mod(x_ref[s], gain, mod_ref[s][0:1], mod_ref[s][1:2]).astype(BF16) for s in streams])
    z = per_stream(_dot(h, win_ref[:, d:2 * d]))
    q = per_stream(_dot(h, win_ref[:, 0:d]))

    k, g_hi, g_lo = [], [], []
    for s in streams:
        e = jnp.exp(-z[s])
        sg = 1.0 / (1.0 + e)
        logf = jnp.log(lb + (1.0 - lb) * sg)
        k.append((1.0 - lb) * (e * sg))
        hi = logf.astype(BF16)
        g_hi.append(hi)
        g_lo.append((logf - hi.astype(F32)).astype(BF16))

    b = [_dot(tri, g_hi[s]) + _dot(tri, g_lo[s]) for s in streams]

    v = per_stream(_dot(h, win_ref[:, 2 * d:3 * d]))
    og = per_stream(_dot(h, win_ref[:, 3 * d:4 * d]))

    span = jnp.zeros((1, d), F32)
    for s in streams:
        b_ref[s] = b[s]
        qk_ref[s, 0] = q[s]
        qk_ref[s, 1] = k[s]
        qt_ref[s] = (q[s] * jnp.exp(b[s])).astype(BF16)
        kt_ref[s] = (k[s] * jnp.exp(-b[s])).astype(BF16)
        for cidx in range(n_chunks):
            rows = slice(cidx * CHUNK, (cidx + 1) * CHUNK)
            bl = b[s][(cidx + 1) * CHUNK - 1:(cidx + 1) * CHUNK]
            kh_ref[s, rows, :] = (k[s][rows] * jnp.exp(bl - b[s][rows])).astype(BF16)
            dl_ref[s, cidx:cidx + 1, :] = jnp.exp(bl)
            span = jnp.maximum(span, -bl)
    fast = jnp.max(span) <= MAX_SPAN
    for s in streams:
        v_ref[s] = v[s].astype(BF16)
        qk_ref[s, 2] = v[s]
        gate_ref[s] = og[s] * jax.nn.sigmoid(og[s])

    n_blocks = tt // SUPER
    cpb = SUPER // CHUNK
    ri = lax.broadcasted_iota(jnp.int32, (SUPER, SUPER), 0)
    ci = lax.broadcasted_iota(jnp.int32, (SUPER, SUPER), 1)
    causal = (ri // CHUNK == ci // CHUNK) & (ci <= ri)
    zero_rows = jnp.zeros((CHUNK, HEAD_DIM), BF16)
    dcols = {(s, hd): dl_ref[s, :, cols].T for s in streams for hd, cols in enumerate(head_cols)}
    dst = {}

    def intra(sc, s):
        rows = slice(sc * SUPER, (sc + 1) * SUPER)
        atts = [_dot_nt(qt_ref[s, rows, cols], kt_ref[s, rows, cols]) for cols in head_cols]
        for cols, att in zip(head_cols, atts):
            att = jnp.where(causal, att, 0.0).astype(BF16)
            oin_ref[s, rows, cols] = _dot(att, v_ref[s, rows, cols])

    def updates(sc, s):
        rows = slice(sc * SUPER, (sc + 1) * SUPER)
        for hd, cols in enumerate(head_cols):
            vv = v_ref[s, rows, cols]
            spread = jnp.concatenate(
                [jnp.concatenate([vv[c * CHUNK:(c + 1) * CHUNK] if r == c else zero_rows
                                  for r in range(cpb)], axis=0) for c in range(cpb)], axis=1)
            dst[sc, s, hd] = _dot_tn(kh_ref[s, rows, cols], spread)

    def recurrence_step(cidx):
        sc, c = divmod(cidx, cpb)
        crow = slice(cidx * CHUNK, (cidx + 1) * CHUNK)
        for s in streams:
            for hd, cols in enumerate(head_cols):
                oacc_ref[s, crow, cols] = _dot(qt_ref[s, crow, cols], st_ref[s, hd].astype(BF16))
        for s in streams:
            for hd, cols in enumerate(head_cols):
                upd = dst[sc, s, hd][:, c * HEAD_DIM:(c + 1) * HEAD_DIM]
                st_ref[s, hd] = st_ref[s, hd] * dcols[s, hd][:, cidx:cidx + 1] + upd

    fillers = []
    for sc in range(n_blocks):
        fillers += [functools.partial(intra, sc, s) for s in streams]
        if sc + 1 < n_blocks:
            fillers += [functools.partial(updates, sc + 1, s) for s in streams]
    for s in streams:
        updates(0, s)
    for cidx in range(n_chunks):
        recurrence_step(cidx)
        if fillers:
            fillers.pop(0)()
        nxt = cidx + 1
        assert nxt % cpb or nxt == n_chunks or all((nxt // cpb, s, 0) in dst for s in streams)
    for filler in fillers:
        filler()

    @pl.when(jnp.logical_not(fast))
    def _():
        rowi = lax.broadcasted_iota(jnp.int32, (CHUNK, 1), 0)

        def body(cidx, carry):
            rows = pl.ds(pl.multiple_of(cidx * CHUNK, CHUNK), CHUNK)
            for s in streams:
                for cols in head_cols:
                    bb = b_ref[s, rows, cols]
                    qq = qk_ref[s, 0, rows, cols]
                    kk = qk_ref[s, 1, rows, cols]
                    vv = qk_ref[s, 2, rows, cols]
                    acc = jnp.zeros((CHUNK, HEAD_DIM), F32)
                    for j in range(CHUNK):
                        p = qq * kk[j:j + 1] * jnp.exp(jnp.minimum(bb - bb[j:j + 1], 0.0))
                        col = jnp.sum(p, axis=-1, keepdims=True)
                        acc = acc + jnp.where(rowi >= j, col, 0.0) * vv[j:j + 1]
                    oin_ref[s, rows, cols] = acc
            return carry
        lax.fori_loop(0, n_chunks, body, 0)

    hg = hgain_ref[...]
    for s in streams:
        for cols in head_cols:
            o = oacc_ref[s, :, cols] + oin_ref[s, :, cols]
            ms = jnp.mean(o * o, axis=-1, keepdims=True)
            oacc_ref[s, :, cols] = o * lax.rsqrt(ms + EPS) * hg[:, cols] * gate_ref[s, :, cols]
    y = _dot(oacc_ref[...].reshape(ns * tt, d).astype(BF16), wout_ref[...])
    for s in streams:
        o_ref[s] = x_ref[s] + mod_ref[s][2:3] * y[s * tt:(s + 1) * tt]


def _chunk_tri(tt):
    r = jnp.arange(tt)[:, None]
    c = jnp.arange(tt)[None, :]
    return ((r // CHUNK == c // CHUNK) & (c <= r)).astype(BF16)


def _hgrn_layer(x, mod, layer, gain, lb_logits, row, w_in, hgain, w_out, *, tt=256):
    bsz, t, d = x.shape
    ns = HGRN_STREAMS
    heads = d // HEAD_DIM
    n_chunks = tt // CHUNK
    kern = functools.partial(_hgrn_kernel, row=row)
    return pl.pallas_call(
        kern,
        grid=(bsz // ns, t // tt),
        in_specs=[
            pl.BlockSpec((ns, tt, d), lambda b, i: (b, i, 0)),
            pl.BlockSpec((None, ns, 6, d), lambda b, i: (layer, b, 0, 0)),
            _layer_spec(gain, layer),
            _const_spec(lb_logits.shape),
            _const_spec((tt, tt)),
            _layer_spec(w_in, row),
            _layer_spec(hgain, row),
            _layer_spec(w_out, row),
        ],
        out_specs=pl.BlockSpec((ns, tt, d), lambda b, i: (b, i, 0)),
        out_shape=jax.ShapeDtypeStruct(x.shape, F32),
        scratch_shapes=[
            pltpu.VMEM((ns, heads, HEAD_DIM, HEAD_DIM), F32),
            pltpu.VMEM((ns, tt, d), BF16),
            pltpu.VMEM((ns, tt, d), BF16),
            pltpu.VMEM((ns, tt, d), BF16),
            pltpu.VMEM((ns, tt, d), BF16),
            pltpu.VMEM((ns, tt, d), F32),
            pltpu.VMEM((ns, 3, tt, d), F32),
            pltpu.VMEM((ns, HEAD_DIM, d), F32),
            pltpu.VMEM((ns, tt, d), F32),
            pltpu.VMEM((ns, tt, d), F32),
            pltpu.VMEM((ns, tt, d), F32),
        ],
        compiler_params=pltpu.CompilerParams(
            dimension_semantics=("parallel", "arbitrary"),
            vmem_limit_bytes=VMEM_LIMIT),
        name="hgrn_mixer",
    )(x, mod, gain, lb_logits, _chunk_tri(tt), w_in, hgain, w_out)


def kernel(x, c, norm_mix_gain, norm_ffn_gain, ada_w, ada_b, pool_w, pool_scale, hgrn_w_in,
           hgrn_lb_logits, hgrn_norm_gain, hgrn_w_out, ffn_w_in, ffn_w_out, final_gain):
    depth = ada_w.shape[0]
    bsz, _, d = x.shape
    n_mixers = 2
    mod = _ada_mod(c, ada_w, ada_b).reshape(depth, bsz, 6, d)

    def rows(a):
        return a.reshape(a.shape[0], 1, d)

    mix_gain, ffn_gain = rows(norm_mix_gain), rows(norm_ffn_gain)
    pool_scale, hgrn_gain = rows(pool_scale), rows(hgrn_norm_gain)
    pool_w, hgrn_w_in, hgrn_w_out, ffn_w_in, ffn_w_out = (
        w.astype(BF16) for w in (pool_w, hgrn_w_in, hgrn_w_out, ffn_w_in, ffn_w_out))
    fgain = final_gain.reshape(1, d)
    for i in range(depth):
        j = i // n_mixers
        pool = None
        if i % n_mixers == 0:
            pool = (mix_gain, pool_w, pool_scale, j)
        else:
            x = _hgrn_layer(x, mod, i, mix_gain, hgrn_lb_logits, j, hgrn_w_in, hgrn_gain, hgrn_w_out)
        x = _ffn_layer(x, mod, i, ffn_gain, ffn_w_in, ffn_w_out, fgain, final=(i == depth - 1),
                       pool=pool)
    return x
```

```python
import functools

import jax
import jax.numpy as jnp
from jax import lax
from jax.experimental import pallas as pl
from jax.experimental.pallas import tpu as pltpu

EPS = 1e-6
POOL_WINDOWS = (2, 4, 8, 16)
POOL_HALO = 16
POOL_BLOCK = 128
POOL_BAND = 256
HEAD_DIM = 128
CHUNK = 64
SUPER = 128
MAX_SPAN = 64.0
HGRN_STREAMS = 2
VMEM_LIMIT = 56 * 1024 * 1024

F32 = jnp.float32
BF16 = jnp.bfloat16


def _dot(a, b):
    return jnp.dot(a, b, preferred_element_type=F32)


def _dot_nt(a, b):
    return lax.dot_general(a, b, (((1,), (1,)), ((), ())), preferred_element_type=F32)


def _dot_tn(a, b):
    return lax.dot_general(a, b, (((0,), (0,)), ((), ())), preferred_element_type=F32)


def _const_spec(shape):
    nd = len(shape)
    return pl.BlockSpec(shape, lambda *_: (0,) * nd)


def _layer_spec(arr, layer):
    nd = arr.ndim - 1
    return pl.BlockSpec((None,) + arr.shape[1:], lambda *_: (layer,) + (0,) * nd)


def _norm_mod(x, gain, shift, scale):
    ms = jnp.mean(x * x, axis=-1, keepdims=True)
    return (x * lax.rsqrt(ms + EPS)) * (gain * (1.0 + scale)) + shift


def _ada_kernel(c_ref, w_ref, b_ref, o_ref):
    c = c_ref[...]
    ca = (c * jax.nn.sigmoid(c)).astype(BF16)
    o_ref[...] = _dot(ca, w_ref[...].astype(BF16)) + b_ref[...]


def _ada_mod(c, ada_w, ada_b):
    depth, d, n = ada_w.shape
    bsz = c.shape[0]
    tn = n // 4
    return pl.pallas_call(
        _ada_kernel,
        grid=(depth, n // tn),
        in_specs=[
            pl.BlockSpec((bsz, d), lambda l, j: (0, 0)),
            pl.BlockSpec((None, d, tn), lambda l, j: (l, 0, j)),
            pl.BlockSpec((None, 1, tn), lambda l, j: (l, 0, j)),
        ],
        out_specs=pl.BlockSpec((None, bsz, tn), lambda l, j: (l, 0, j)),
        out_shape=jax.ShapeDtypeStruct((depth, bsz, n), F32),
        compiler_params=pltpu.CompilerParams(
            dimension_semantics=("arbitrary", "arbitrary"),
            vmem_limit_bytes=VMEM_LIMIT),
        name="ada_mod",
    )(c, ada_w, ada_b.reshape(depth, 1, n))


def _pool_bands():
    i = jnp.arange(POOL_BLOCK)[:, None] + (POOL_BAND - POOL_BLOCK)
    j = jnp.arange(POOL_BAND)[None, :]
    return jnp.stack([((j <= i) & (j > i - w)) for w in POOL_WINDOWS]).astype(BF16)


def _split_bf16(a):
    hi = a.astype(BF16)
    return hi, (a - hi.astype(F32)).astype(BF16)


def _ffn_kernel(*refs, fc, final, pool, n_sub):
    if pool:
        (x_ref, halo_ref, mod_ref, mgain_ref, band_ref, pw_ref, ps_ref, gain_ref, win_ref,
         wout_ref, fgain_ref, o_ref, act_ref, x1_ref, hf_ref, hi_ref, lo_ref, d_ref) = refs
    else:
        x_ref, mod_ref, gain_ref, win_ref, wout_ref, fgain_ref, o_ref, act_ref = refs
    i = pl.program_id(1)
    tm, d = x_ref.shape
    f = wout_ref.shape[0]
    ts = tm // n_sub
    m = mod_ref[...]
    pad = POOL_BAND - POOL_BLOCK

    if pool:
        gw = d // len(POOL_WINDOWS)
        hh = jnp.where(i == 0, 0.0, _norm_mod(halo_ref[...], mgain_ref[...], m[0:1], m[1:2]))
        for ref, top in zip((hi_ref, lo_ref), _split_bf16(hh)):
            ref[0:pad - POOL_HALO, :] = jnp.zeros((pad - POOL_HALO, d), BF16)
            ref[pad - POOL_HALO:pad, :] = top

    def pool_split(rows):
        h = _norm_mod(x_ref[rows, :], mgain_ref[...], m[0:1], m[1:2])
        hf_ref[rows, :] = h
        for ref, piece in zip((hi_ref, lo_ref), _split_bf16(h)):
            ref[pad + rows.start:pad + rows.stop, :] = piece

    def pool_group(rows, g):
        w = POOL_WINDOWS[g]
        cols = slice(g * gw, (g + 1) * gw)
        n = rows.stop - rows.start
        pos = i * tm + rows.start + lax.broadcasted_iota(jnp.int32, (n, 1), 0)
        inv_cnt = 1.0 / jnp.minimum(pos + 1, w).astype(F32)
        for blk in range(n // POOL_BLOCK):
            r0 = rows.start + blk * POOL_BLOCK
            win = slice(r0, r0 + POOL_BAND)
            out = slice(r0, r0 + POOL_BLOCK)
            sm = _dot(band_ref[g], hi_ref[win, cols]) + _dot(band_ref[g], lo_ref[win, cols])
            loc = slice(blk * POOL_BLOCK, (blk + 1) * POOL_BLOCK)
            d_ref[out, cols] = (sm * inv_cnt[loc] - hf_ref[out, cols]).astype(BF16)
        y = _dot(d_ref[rows, cols], pw_ref[g]) * ps_ref[:, cols]
        x1_ref[rows, cols] = x_ref[rows, cols] + m[2:3, cols] * y

    src = x1_ref if pool else x_ref
    hs = []
    for j in range(n_sub):
        rows = slice(j * ts, (j + 1) * ts)
        if pool:
            pool_split(rows)
            for g in range(len(POOL_WINDOWS)):
                pool_group(rows, g)
        hs.append(_norm_mod(src[rows, :], gain_ref[...], m[3:4], m[4:5]).astype(BF16))
    for j in range(n_sub):
        rows = slice(j * ts, (j + 1) * ts)
        for c in range(f // fc):
            a = _dot(hs[j], win_ref[:, c * fc:(c + 1) * fc])
            b = _dot(hs[j], win_ref[:, f + c * fc:f + (c + 1) * fc])
            act_ref[rows, c * fc:(c + 1) * fc] = (a * jax.nn.sigmoid(a) * b).astype(BF16)
        y = _dot(act_ref[rows, :], wout_ref[...])
        out = src[rows, :] + m[5:6] * y
        if final:
            ms = jnp.mean(out * out, axis=-1, keepdims=True)
            out = out * lax.rsqrt(ms + EPS) * fgain_ref[...]
        o_ref[rows, :] = out


def _ffn_layer(x, mod, layer, gain, w_in, w_out, final_gain, *, final, pool=None,
               tm=512, fc=256, n_sub=2):
    bsz, t, d = x.shape
    f = w_out.shape[1]
    kern = functools.partial(_ffn_kernel, fc=fc, final=final, pool=pool is not None, n_sub=n_sub)
    x_spec = pl.BlockSpec((None, tm, d), lambda b, i: (b, i, 0))
    mod_spec = pl.BlockSpec((None, None, 6, d), lambda b, i: (layer, b, 0, 0))
    ffn_specs = [_layer_spec(gain, layer), _layer_spec(w_in, layer), _layer_spec(w_out, layer),
                 _const_spec((1, d))]
    ffn_args = [gain, w_in, w_out, final_gain]
    scratch = [pltpu.VMEM((tm, f), BF16)]
    if pool is None:
        in_specs = [x_spec, mod_spec] + ffn_specs
        args = [x, mod] + ffn_args
    else:
        mix_gain, pool_w, pool_scale, row = pool
        hb = tm // POOL_HALO
        bands = _pool_bands()
        halo_spec = pl.BlockSpec((None, POOL_HALO, d),
                                 lambda b, i: (b, jnp.maximum(i * hb - 1, 0), 0))
        in_specs = [x_spec, halo_spec, mod_spec, _layer_spec(mix_gain, layer),
                    _const_spec(bands.shape), _layer_spec(pool_w, row),
                    _layer_spec(pool_scale, row)] + ffn_specs
        args = [x, x, mod, mix_gain, bands, pool_w, pool_scale] + ffn_args
        ext = tm + POOL_BAND - POOL_BLOCK
        scratch += [
            pltpu.VMEM((tm, d), F32),
            pltpu.VMEM((tm, d), F32),
            pltpu.VMEM((ext, d), BF16),
            pltpu.VMEM((ext, d), BF16),
            pltpu.VMEM((tm, d), BF16),
        ]
    return pl.pallas_call(
        kern,
        grid=(bsz, t // tm),
        in_specs=in_specs,
        out_specs=pl.BlockSpec((None, tm, d), lambda b, i: (b, i, 0)),
        out_shape=jax.ShapeDtypeStruct(x.shape, F32),
        scratch_shapes=scratch,
        compiler_params=pltpu.CompilerParams(
            dimension_semantics=("parallel", "parallel"),
            vmem_limit_bytes=VMEM_LIMIT),
        name="pool_ffn" if pool is not None else "ffn",
    )(*args)


def _lower_bound(logits, row):
    mx = jnp.max(logits, axis=0, keepdims=True)
    e = jnp.exp(logits - mx)
    sm = e / jnp.sum(e, axis=0, keepdims=True)
    acc = sm[0:1]
    for r in range(1, row + 1):
        acc = acc + sm[r:r + 1]
    return acc - sm[0:1]


def _hgrn_kernel(x_ref, mod_ref, gain_ref, lbl_ref, tri_ref, win_ref, hgain_ref, wout_ref, o_ref,
                 st_ref, qi_ref, qa_ref, ka_ref, kh_ref, v_ref, b_ref, qk_ref, dl_ref, gate_ref,
                 oacc_ref, oin_ref, *, row):
    i = pl.program_id(1)
    ns, tt, d = x_ref.shape
    streams = range(ns)
    head_cols = [slice(hd * HEAD_DIM, (hd + 1) * HEAD_DIM) for hd in range(d // HEAD_DIM)]
    n_chunks = tt // CHUNK

    @pl.when(i == 0)
    def _():
        st_ref[...] = jnp.zeros_like(st_ref)
        dl_ref[...] = jnp.zeros_like(dl_ref)

    gain = gain_ref[...]
    lb = _lower_bound(lbl_ref[...], row)
    tri = tri_ref[...]

    def per_stream(a):
        return [a[s * tt:(s + 1) * tt] for s in streams]

    h = jnp.concatenate(
        [_norm_mod(x_ref[s], gain, mod_ref[s][0:1], mod_ref[s][1:2]).astype(BF16) for s in streams])
    z = per_stream(_dot(h, win_ref[:, d:2 * d]))
    q = per_stream(_dot(h, win_ref[:, 0:d]))

    k, g_hi, g_lo = [], [], []
    for s in streams:
        e = jnp.exp(-z[s])
        sg = 1.0 / (1.0 + e)
        logf = jnp.log(lb + (1.0 - lb) * sg)
        k.append((1.0 - lb) * (e * sg))
        hi = logf.astype(BF16)
        g_hi.append(hi)
        g_lo.append((logf - hi.astype(F32)).astype(BF16))

    b = [_dot(tri, g_hi[s]) + _dot(tri, g_lo[s]) for s in streams]

    v = per_stream(_dot(h, win_ref[:, 2 * d:3 * d]))
    og = per_stream(_dot(h, win_ref[:, 3 * d:4 * d]))

    mid = CHUNK // 2 - 1
    span = jnp.zeros((1, d), F32)
    for s in streams:
        b_ref[s] = b[s]
        qk_ref[s, 0] = q[s]
        qk_ref[s, 1] = k[s]
        qi_ref[s] = (q[s] * jnp.exp(b[s])).astype(BF16)
        for cidx in range(n_chunks):
            rows = slice(cidx * CHUNK, (cidx + 1) * CHUNK)
            bc = b[s][rows]
            bm = bc[mid:mid + 1]
            bl = bc[CHUNK - 1:CHUNK]
            qa_ref[s, rows, :] = (q[s][rows] * jnp.exp(bc - bm)).astype(BF16)
            ka_ref[s, rows, :] = (k[s][rows] * jnp.exp(bm - bc)).astype(BF16)
            kh_ref[s, rows, :] = (k[s][rows] * jnp.exp(bl - bc)).astype(BF16)
            dl_ref[s, cidx:cidx + 1, :] = jnp.exp(bl)
            span = jnp.maximum(span, jnp.maximum(-bm, bm - bl))
    fast = jnp.max(span) <= MAX_SPAN
    for s in streams:
        v_ref[s] = v[s].astype(BF16)
        qk_ref[s, 2] = v[s]
        gate_ref[s] = og[s] * jax.nn.sigmoid(og[s])

    n_blocks = tt // SUPER
    cpb = SUPER // CHUNK
    ri = lax.broadcasted_iota(jnp.int32, (SUPER, SUPER), 0)
    ci = lax.broadcasted_iota(jnp.int32, (SUPER, SUPER), 1)
    causal = (ri // CHUNK == ci // CHUNK) & (ci <= ri)
    zero_rows = jnp.zeros((CHUNK, HEAD_DIM), BF16)
    dcols = {(s, hd): dl_ref[s, :, cols].T for s in streams for hd, cols in enumerate(head_cols)}
    dst = {}

    def intra(sc, s):
        rows = slice(sc * SUPER, (sc + 1) * SUPER)
        atts = [_dot_nt(qa_ref[s, rows, cols], ka_ref[s, rows, cols]) for cols in head_cols]
        for cols, att in zip(head_cols, atts):
            att = jnp.where(causal, att, 0.0).astype(BF16)
            oin_ref[s, rows, cols] = _dot(att, v_ref[s, rows, cols])

    def updates(sc, s):
        rows = slice(sc * SUPER, (sc + 1) * SUPER)
        for hd, cols in enumerate(head_cols):
            vv = v_ref[s, rows, cols]
            spread = jnp.concatenate(
                [jnp.concatenate([vv[c * CHUNK:(c + 1) * CHUNK] if r == c else zero_rows
                                  for r in range(cpb)], axis=0) for c in range(cpb)], axis=1)
            dst[sc, s, hd] = _dot_tn(kh_ref[s, rows, cols], spread)

    def recurrence_step(cidx):
        sc, c = divmod(cidx, cpb)
        crow = slice(cidx * CHUNK, (cidx + 1) * CHUNK)
        for s in streams:
            for hd, cols in enumerate(head_cols):
                oacc_ref[s, crow, cols] = _dot(qi_ref[s, crow, cols], st_ref[s, hd].astype(BF16))
        for s in streams:
            for hd, cols in enumerate(head_cols):
                upd = dst[sc, s, hd][:, c * HEAD_DIM:(c + 1) * HEAD_DIM]
                st_ref[s, hd] = st_ref[s, hd] * dcols[s, hd][:, cidx:cidx + 1] + upd

    fillers = []
    for sc in range(n_blocks):
        if sc + 1 < n_blocks:
            fillers += [functools.partial(updates, sc + 1, s) for s in streams]
        fillers += [functools.partial(intra, sc, s) for s in streams]
    for s in streams:
        updates(0, s)
    for cidx in range(n_chunks):
        recurrence_step(cidx)
        if fillers:
            fillers.pop(0)()
        nxt = cidx + 1
        assert nxt % cpb or nxt == n_chunks or all((nxt // cpb, s, 0) in dst for s in streams)
    for filler in fillers:
        filler()

    @pl.when(jnp.logical_not(fast))
    def _():
        rowi = lax.broadcasted_iota(jnp.int32, (CHUNK, 1), 0)

        def body(cidx, carry):
            rows = pl.ds(pl.multiple_of(cidx * CHUNK, CHUNK), CHUNK)
            for s in streams:
                for cols in head_cols:
                    bb = b_ref[s, rows, cols]
                    qq = qk_ref[s, 0, rows, cols]
                    kk = qk_ref[s, 1, rows, cols]
                    vv = qk_ref[s, 2, rows, cols]
                    acc = jnp.zeros((CHUNK, HEAD_DIM), F32)
                    for j in range(CHUNK):
                        p = qq * kk[j:j + 1] * jnp.exp(jnp.minimum(bb - bb[j:j + 1], 0.0))
                        col = jnp.sum(p, axis=-1, keepdims=True)
                        acc = acc + jnp.where(rowi >= j, col, 0.0) * vv[j:j + 1]
                    oin_ref[s, rows, cols] = acc
            return carry
        lax.fori_loop(0, n_chunks, body, 0)

    hg = hgain_ref[...]
    for s in streams:
        for cols in head_cols:
            o = oacc_ref[s, :, cols] + oin_ref[s, :, cols]
            ms = jnp.mean(o * o, axis=-1, keepdims=True)
            oacc_ref[s, :, cols] = o * lax.rsqrt(ms + EPS) * hg[:, cols] * gate_ref[s, :, cols]
    y = _dot(oacc_ref[...].reshape(ns * tt, d).astype(BF16), wout_ref[...])
    for s in streams:
        o_ref[s] = x_ref[s] + mod_ref[s][2:3] * y[s * tt:(s + 1) * tt]


def _chunk_tri(tt):
    r = jnp.arange(tt)[:, None]
    c = jnp.arange(tt)[None, :]
    return ((r // CHUNK == c // CHUNK) & (c <= r)).astype(BF16)


def _hgrn_layer(x, mod, layer, gain, lb_logits, row, w_in, hgain, w_out, *, tt=256):
    bsz, t, d = x.shape
    ns = HGRN_STREAMS
    heads = d // HEAD_DIM
    n_chunks = tt // CHUNK
    kern = functools.partial(_hgrn_kernel, row=row)
    return pl.pallas_call(
        kern,
        grid=(bsz // ns, t // tt),
        in_specs=[
            pl.BlockSpec((ns, tt, d), lambda b, i: (b, i, 0)),
            pl.BlockSpec((None, ns, 6, d), lambda b, i: (layer, b, 0, 0)),
            _layer_spec(gain, layer),
            _const_spec(lb_logits.shape),
            _const_spec((tt, tt)),
            _layer_spec(w_in, row),
            _layer_spec(hgain, row),
            _layer_spec(w_out, row),
        ],
        out_specs=pl.BlockSpec((ns, tt, d), lambda b, i: (b, i, 0)),
        out_shape=jax.ShapeDtypeStruct(x.shape, F32),
        scratch_shapes=[
            pltpu.VMEM((ns, heads, HEAD_DIM, HEAD_DIM), F32),
            pltpu.VMEM((ns, tt, d), BF16),
            pltpu.VMEM((ns, tt, d), BF16),
            pltpu.VMEM((ns, tt, d), BF16),
            pltpu.VMEM((ns, tt, d), BF16),
            pltpu.VMEM((ns, tt, d), BF16),
            pltpu.VMEM((ns, tt, d), F32),
            pltpu.VMEM((ns, 3, tt, d), F32),
            pltpu.VMEM((ns, HEAD_DIM, d), F32),
            pltpu.VMEM((ns, tt, d), F32),
            pltpu.VMEM((ns, tt, d), F32),
            pltpu.VMEM((ns, tt, d), F32),
        ],
        compiler_params=pltpu.CompilerParams(
            dimension_semantics=("parallel", "arbitrary"),
            vmem_limit_bytes=VMEM_LIMIT),
        name="hgrn_mixer",
    )(x, mod, gain, lb_logits, _chunk_tri(tt), w_in, hgain, w_out)


def kernel(x, c, norm_mix_gain, norm_ffn_gain, ada_w, ada_b, pool_w, pool_scale, hgrn_w_in,
           hgrn_lb_logits, hgrn_norm_gain, hgrn_w_out, ffn_w_in, ffn_w_out, final_gain):
    depth = ada_w.shape[0]
    bsz, _, d = x.shape
    n_mixers = 2
    mod = _ada_mod(c, ada_w, ada_b).reshape(depth, bsz, 6, d)

    def rows(a):
        return a.reshape(a.shape[0], 1, d)

    mix_gain, ffn_gain = rows(norm_mix_gain), rows(norm_ffn_gain)
    pool_scale, hgrn_gain = rows(pool_scale), rows(hgrn_norm_gain)
    pool_w, hgrn_w_in, hgrn_w_out, ffn_w_in, ffn_w_out = (
        w.astype(BF16) for w in (pool_w, hgrn_w_in, hgrn_w_out, ffn_w_in, ffn_w_out))
    fgain = final_gain.reshape(1, d)
    for i in range(depth):
        j = i // n_mixers
        pool = None
        if i % n_mixers == 0:
            pool = (mix_gain, pool_w, pool_scale, j)
        else:
            x = _hgrn_layer(x, mod, i, mix_gain, hgrn_lb_logits, j, hgrn_w_in, hgrn_gain, hgrn_w_out)
        x = _ffn_layer(x, mod, i, ffn_gain, ffn_w_in, ffn_w_out, fgain, final=(i == depth - 1),
                       pool=pool)
    return x
```

```python
import functools

import jax
import jax.numpy as jnp
from jax import lax
from jax.experimental import pallas as pl
from jax.experimental.pallas import tpu as pltpu

EPS = 1e-6
POOL_WINDOWS = (2, 4, 8, 16)
POOL_HALO = 16
POOL_BLOCK = 128
POOL_BAND = 256
SUBLANES = 8
HEAD_DIM = 128
CHUNK = 64
SUPER = 128
MAX_SPAN = 64.0
HGRN_STREAMS = 2
VMEM_LIMIT = 56 * 1024 * 1024

F32 = jnp.float32
BF16 = jnp.bfloat16


def _dot(a, b):
    return jnp.dot(a, b, preferred_element_type=F32)


def _dot_nt(a, b):
    return lax.dot_general(a, b, (((1,), (1,)), ((), ())), preferred_element_type=F32)


def _dot_tn(a, b):
    return lax.dot_general(a, b, (((0,), (0,)), ((), ())), preferred_element_type=F32)


def _const_spec(shape):
    nd = len(shape)
    return pl.BlockSpec(shape, lambda *_: (0,) * nd)


def _layer_spec(arr, layer):
    nd = arr.ndim - 1
    return pl.BlockSpec((None,) + arr.shape[1:], lambda *_: (layer,) + (0,) * nd)


def _norm_mod(x, gain, shift, scale):
    ms = jnp.mean(x * x, axis=-1, keepdims=True)
    return (x * lax.rsqrt(ms + EPS)) * (gain * (1.0 + scale)) + shift


def _ada_kernel(c_ref, w_ref, b_ref, o_ref):
    c = c_ref[...]
    ca = (c * jax.nn.sigmoid(c)).astype(BF16)
    o_ref[...] = _dot(ca, w_ref[...].astype(BF16)) + b_ref[...]


def _ada_mod(c, ada_w, ada_b):
    depth, d, n = ada_w.shape
    bsz = c.shape[0]
    tn = n // 4
    return pl.pallas_call(
        _ada_kernel,
        grid=(depth, n // tn),
        in_specs=[
            pl.BlockSpec((bsz, d), lambda l, j: (0, 0)),
            pl.BlockSpec((None, d, tn), lambda l, j: (l, 0, j)),
            pl.BlockSpec((None, 1, tn), lambda l, j: (l, 0, j)),
        ],
        out_specs=pl.BlockSpec((None, bsz, tn), lambda l, j: (l, 0, j)),
        out_shape=jax.ShapeDtypeStruct((depth, bsz, n), F32),
        compiler_params=pltpu.CompilerParams(
            dimension_semantics=("arbitrary", "arbitrary"),
            vmem_limit_bytes=VMEM_LIMIT),
        name="ada_mod",
    )(c, ada_w, ada_b.reshape(depth, 1, n))


def _pool_bands():
    i = jnp.arange(POOL_BLOCK)[:, None] + (POOL_BAND - POOL_BLOCK)
    j = jnp.arange(POOL_BAND)[None, :]
    return jnp.stack([((j <= i) & (j > i - w)) for w in POOL_WINDOWS]).astype(BF16)


def _split_bf16(a):
    hi = a.astype(BF16)
    return hi, (a - hi.astype(F32)).astype(BF16)


def _ffn_kernel(*refs, fc, final, pool, n_sub):
    if pool:
        (x_ref, halo_ref, mod_ref, mgain_ref, band_ref, pw_ref, ps_ref, gain_ref, win_ref,
         wout_ref, fgain_ref, o_ref, act_ref, x1_ref, hf_ref, hi_ref, lo_ref, d_ref) = refs
    else:
        x_ref, mod_ref, gain_ref, win_ref, wout_ref, fgain_ref, o_ref, act_ref = refs
    i = pl.program_id(1)
    tm, d = x_ref.shape
    f = wout_ref.shape[0]
    ts = tm // n_sub
    m = mod_ref[...]
    pad = POOL_BAND - POOL_BLOCK

    if pool:
        gw = d // len(POOL_WINDOWS)
        hh = jnp.where(i == 0, 0.0, _norm_mod(halo_ref[...], mgain_ref[...], m[0:1], m[1:2]))
        for ref, top in zip((hi_ref, lo_ref), _split_bf16(hh)):
            ref[0:pad - POOL_HALO, :] = jnp.zeros((pad - POOL_HALO, d), BF16)
            ref[pad - POOL_HALO:pad, :] = top

    def pool_split(rows):
        h = _norm_mod(x_ref[rows, :], mgain_ref[...], m[0:1], m[1:2])
        hf_ref[rows, :] = h
        for ref, piece in zip((hi_ref, lo_ref), _split_bf16(h)):
            ref[pad + rows.start:pad + rows.stop, :] = piece

    def pool_group(rows, g):
        w = POOL_WINDOWS[g]
        cols = slice(g * gw, (g + 1) * gw)
        n = rows.stop - rows.start
        pos = i * tm + rows.start + lax.broadcasted_iota(jnp.int32, (n, 1), 0)
        inv_cnt = 1.0 / jnp.minimum(pos + 1, w).astype(F32)
        for blk in range(n // POOL_BLOCK):
            r0 = rows.start + blk * POOL_BLOCK
            win = slice(r0, r0 + POOL_BAND)
            out = slice(r0, r0 + POOL_BLOCK)
            sm = _dot(band_ref[g], hi_ref[win, cols]) + _dot(band_ref[g], lo_ref[win, cols])
            loc = slice(blk * POOL_BLOCK, (blk + 1) * POOL_BLOCK)
            d_ref[out, cols] = (sm * inv_cnt[loc] - hf_ref[out, cols]).astype(BF16)
        y = _dot(d_ref[rows, cols], pw_ref[g]) * ps_ref[:, cols]
        x1_ref[rows, cols] = x_ref[rows, cols] + m[2:3, cols] * y

    src = x1_ref if pool else x_ref
    hs = []
    for j in range(n_sub):
        rows = slice(j * ts, (j + 1) * ts)
        if pool:
            pool_split(rows)
            for g in range(len(POOL_WINDOWS)):
                pool_group(rows, g)
        hs.append(_norm_mod(src[rows, :], gain_ref[...], m[3:4], m[4:5]).astype(BF16))
    for j in range(n_sub):
        rows = slice(j * ts, (j + 1) * ts)
        for c in range(f // fc):
            a = _dot(hs[j], win_ref[:, c * fc:(c + 1) * fc])
            b = _dot(hs[j], win_ref[:, f + c * fc:f + (c + 1) * fc])
            act_ref[rows, c * fc:(c + 1) * fc] = (a * jax.nn.sigmoid(a) * b).astype(BF16)
        y = _dot(act_ref[rows, :], wout_ref[...])
        out = src[rows, :] + m[5:6] * y
        if final:
            ms = jnp.mean(out * out, axis=-1, keepdims=True)
            out = out * lax.rsqrt(ms + EPS) * fgain_ref[...]
        o_ref[rows, :] = out


def _ffn_layer(x, mod, layer, gain, w_in, w_out, final_gain, *, final, pool=None,
               tm=512, fc=256, n_sub=2):
    bsz, t, d = x.shape
    f = w_out.shape[1]
    kern = functools.partial(_ffn_kernel, fc=fc, final=final, pool=pool is not None, n_sub=n_sub)
    x_spec = pl.BlockSpec((None, tm, d), lambda b, i: (b, i, 0))
    mod_spec = pl.BlockSpec((None, None, 6, d), lambda b, i: (layer, b, 0, 0))
    ffn_specs = [_layer_spec(gain, layer), _layer_spec(w_in, layer), _layer_spec(w_out, layer),
                 _const_spec((1, d))]
    ffn_args = [gain, w_in, w_out, final_gain]
    scratch = [pltpu.VMEM((tm, f), BF16)]
    if pool is None:
        in_specs = [x_spec, mod_spec] + ffn_specs
        args = [x, mod] + ffn_args
    else:
        mix_gain, pool_w, pool_scale, row = pool
        hb = tm // POOL_HALO
        bands = _pool_bands()
        halo_spec = pl.BlockSpec((None, POOL_HALO, d),
                                 lambda b, i: (b, jnp.maximum(i * hb - 1, 0), 0))
        in_specs = [x_spec, halo_spec, mod_spec, _layer_spec(mix_gain, layer),
                    _const_spec(bands.shape), _layer_spec(pool_w, row),
                    _layer_spec(pool_scale, row)] + ffn_specs
        args = [x, x, mod, mix_gain, bands, pool_w, pool_scale] + ffn_args
        ext = tm + POOL_BAND - POOL_BLOCK
        scratch += [
            pltpu.VMEM((tm, d), F32),
            pltpu.VMEM((tm, d), F32),
            pltpu.VMEM((ext, d), BF16),
            pltpu.VMEM((ext, d), BF16),
            pltpu.VMEM((tm, d), BF16),
        ]
    return pl.pallas_call(
        kern,
        grid=(bsz, t // tm),
        in_specs=in_specs,
        out_specs=pl.BlockSpec((None, tm, d), lambda b, i: (b, i, 0)),
        out_shape=jax.ShapeDtypeStruct(x.shape, F32),
        scratch_shapes=scratch,
        compiler_params=pltpu.CompilerParams(
            dimension_semantics=("parallel", "parallel"),
            vmem_limit_bytes=VMEM_LIMIT),
        name="pool_ffn" if pool is not None else "ffn",
    )(*args)


def _lower_bound(logits, row):
    mx = jnp.max(logits, axis=0, keepdims=True)
    e = jnp.exp(logits - mx)
    sm = e / jnp.sum(e, axis=0, keepdims=True)
    acc = sm[0:1]
    for r in range(1, row + 1):
        acc = acc + sm[r:r + 1]
    return acc - sm[0:1]


def _hgrn_kernel(x_ref, mod_ref, gain_ref, lbl_ref, tri_ref, win_ref, hgain_ref, wout_ref, o_ref,
                 st_ref, qi_ref, qa_ref, ka_ref, kh_ref, v_ref, b_ref, qk_ref, dl_ref, gate_ref,
                 oacc_ref, oin_ref, *, row):
    i = pl.program_id(1)
    ns, tt, d = x_ref.shape
    streams = range(ns)
    head_cols = [slice(hd * HEAD_DIM, (hd + 1) * HEAD_DIM) for hd in range(d // HEAD_DIM)]
    n_chunks = tt // CHUNK

    @pl.when(i == 0)
    def _():
        st_ref[...] = jnp.zeros_like(st_ref)
        dl_ref[...] = jnp.zeros_like(dl_ref)

    gain = gain_ref[...]
    lb = _lower_bound(lbl_ref[...], row)
    tri = tri_ref[...]

    def per_stream(a):
        return [a[s * tt:(s + 1) * tt] for s in streams]

    h = jnp.concatenate(
        [_norm_mod(x_ref[s], gain, mod_ref[s][0:1], mod_ref[s][1:2]).astype(BF16) for s in streams])
    z = per_stream(_dot(h, win_ref[:, d:2 * d]))
    q = per_stream(_dot(h, win_ref[:, 0:d]))

    k, g_hi, g_lo = [], [], []
    for s in streams:
        e = jnp.exp(-z[s])
        sg = 1.0 / (1.0 + e)
        logf = jnp.log(lb + (1.0 - lb) * sg)
        k.append((1.0 - lb) * (e * sg))
        hi = logf.astype(BF16)
        g_hi.append(hi)
        g_lo.append((logf - hi.astype(F32)).astype(BF16))

    b = [_dot(tri, g_hi[s]) + _dot(tri, g_lo[s]) for s in streams]

    v = per_stream(_dot(h, win_ref[:, 2 * d:3 * d]))
    og = per_stream(_dot(h, win_ref[:, 3 * d:4 * d]))

    mid = CHUNK // 2 - 1
    span = jnp.zeros((1, d), F32)
    for s in streams:
        b_ref[s] = b[s]
        qk_ref[s, 0] = q[s]
        qk_ref[s, 1] = k[s]
        qi_ref[s] = (q[s] * jnp.exp(b[s])).astype(BF16)
        for cidx in range(n_chunks):
            rows = slice(cidx * CHUNK, (cidx + 1) * CHUNK)
            bc = b[s][rows]
            bm = bc[mid:mid + 1]
            bl = bc[CHUNK - 1:CHUNK]
            qa_ref[s, rows, :] = (q[s][rows] * jnp.exp(bc - bm)).astype(BF16)
            ka_ref[s, rows, :] = (k[s][rows] * jnp.exp(bm - bc)).astype(BF16)
            kh_ref[s, rows, :] = (k[s][rows] * jnp.exp(bl - bc)).astype(BF16)
            dl_ref[s, cidx:cidx + 1, :] = jnp.exp(bl)
            span = jnp.maximum(span, jnp.maximum(-bm, bm - bl))
    fast = jnp.max(span) <= MAX_SPAN
    for s in streams:
        v_ref[s] = v[s].astype(BF16)
        qk_ref[s, 2] = v[s]
        gate_ref[s] = og[s] * jax.nn.sigmoid(og[s])

    n_blocks = tt // SUPER
    cpb = SUPER // CHUNK
    ri = lax.broadcasted_iota(jnp.int32, (SUPER, SUPER), 0)
    ci = lax.broadcasted_iota(jnp.int32, (SUPER, SUPER), 1)
    causal = (ri // CHUNK == ci // CHUNK) & (ci <= ri)
    zero_rows = jnp.zeros((CHUNK, HEAD_DIM), BF16)
    dcols = {(s, hd): dl_ref[s, :, cols].T for s in streams for hd, cols in enumerate(head_cols)}
    dst = {}

    def intra(sc, s):
        rows = slice(sc * SUPER, (sc + 1) * SUPER)
        atts = [_dot_nt(qa_ref[s, rows, cols], ka_ref[s, rows, cols]) for cols in head_cols]
        for cols, att in zip(head_cols, atts):
            att = jnp.where(causal, att, 0.0).astype(BF16)
            oin_ref[s, rows, cols] = _dot(att, v_ref[s, rows, cols])

    def updates(sc, s):
        rows = slice(sc * SUPER, (sc + 1) * SUPER)
        for hd, cols in enumerate(head_cols):
            vv = v_ref[s, rows, cols]
            spread = jnp.concatenate(
                [jnp.concatenate([vv[c * CHUNK:(c + 1) * CHUNK] if r == c else zero_rows
                                  for r in range(cpb)], axis=0) for c in range(cpb)], axis=1)
            dst[sc, s, hd] = _dot_tn(kh_ref[s, rows, cols], spread)

    def recurrence_step(cidx):
        sc, c = divmod(cidx, cpb)
        crow = slice(cidx * CHUNK, (cidx + 1) * CHUNK)
        for s in streams:
            for hd, cols in enumerate(head_cols):
                oacc_ref[s, crow, cols] = _dot(qi_ref[s, crow, cols], st_ref[s, hd].astype(BF16))
        for s in streams:
            for hd, cols in enumerate(head_cols):
                upd = dst[sc, s, hd][:, c * HEAD_DIM:(c + 1) * HEAD_DIM]
                st_ref[s, hd] = st_ref[s, hd] * dcols[s, hd][:, cidx:cidx + 1] + upd

    fillers = []
    for sc in range(n_blocks):
        if sc + 1 < n_blocks:
            fillers += [functools.partial(updates, sc + 1, s) for s in streams]
        fillers += [functools.partial(intra, sc, s) for s in streams]
    for s in streams:
        updates(0, s)
    for cidx in range(n_chunks):
        recurrence_step(cidx)
        if fillers:
            fillers.pop(0)()
        nxt = cidx + 1
        assert nxt % cpb or nxt == n_chunks or all((nxt // cpb, s, 0) in dst for s in streams)
    for filler in fillers:
        filler()

    @pl.when(jnp.logical_not(fast))
    def _():
        rowi = lax.broadcasted_iota(jnp.int32, (CHUNK, 1), 0)

        def chunk_body(it, carry):
            s, cidx = it // n_chunks, it % n_chunks
            row0 = pl.multiple_of(cidx * CHUNK, CHUNK)
            rows = pl.ds(row0, CHUNK)
            for cols in head_cols:
                bb = b_ref[s, rows, cols]
                qq = qk_ref[s, 0, rows, cols]

                def key_group(jg, acc):
                    krows = pl.ds(pl.multiple_of(row0 + jg * SUBLANES, SUBLANES), SUBLANES)
                    bk = b_ref[s, krows, cols]
                    kk = qk_ref[s, 1, krows, cols]
                    vv = qk_ref[s, 2, krows, cols]
                    for r in range(SUBLANES):
                        p = qq * kk[r:r + 1] * jnp.exp(jnp.minimum(bb - bk[r:r + 1], 0.0))
                        col = jnp.sum(p, axis=-1, keepdims=True)
                        acc = acc + jnp.where(rowi >= jg * SUBLANES + r, col, 0.0) * vv[r:r + 1]
                    return acc

                zero = jnp.zeros((CHUNK, HEAD_DIM), F32)
                oin_ref[s, rows, cols] = lax.fori_loop(0, CHUNK // SUBLANES, key_group, zero)
            return carry
        lax.fori_loop(0, ns * n_chunks, chunk_body, 0)

    hg = hgain_ref[...]
    for s in streams:
        for cols in head_cols:
            o = oacc_ref[s, :, cols] + oin_ref[s, :, cols]
            ms = jnp.mean(o * o, axis=-1, keepdims=True)
            oacc_ref[s, :, cols] = o * lax.rsqrt(ms + EPS) * hg[:, cols] * gate_ref[s, :, cols]
    y = _dot(oacc_ref[...].reshape(ns * tt, d).astype(BF16), wout_ref[...])
    for s in streams:
        o_ref[s] = x_ref[s] + mod_ref[s][2:3] * y[s * tt:(s + 1) * tt]


def _chunk_tri(tt):
    r = jnp.arange(tt)[:, None]
    c = jnp.arange(tt)[None, :]
    return ((r // CHUNK == c // CHUNK) & (c <= r)).astype(BF16)


def _hgrn_layer(x, mod, layer, gain, lb_logits, row, w_in, hgain, w_out, *, tt=256):
    bsz, t, d = x.shape
    ns = HGRN_STREAMS
    heads = d // HEAD_DIM
    n_chunks = tt // CHUNK
    kern = functools.partial(_hgrn_kernel, row=row)
    return pl.pallas_call(
        kern,
        grid=(bsz // ns, t // tt),
        in_specs=[
            pl.BlockSpec((ns, tt, d), lambda b, i: (b, i, 0)),
            pl.BlockSpec((None, ns, 6, d), lambda b, i: (layer, b, 0, 0)),
            _layer_spec(gain, layer),
            _const_spec(lb_logits.shape),
            _const_spec((tt, tt)),
            _layer_spec(w_in, row),
            _layer_spec(hgain, row),
            _layer_spec(w_out, row),
        ],
        out_specs=pl.BlockSpec((ns, tt, d), lambda b, i: (b, i, 0)),
        out_shape=jax.ShapeDtypeStruct(x.shape, F32),
        scratch_shapes=[
            pltpu.VMEM((ns, heads, HEAD_DIM, HEAD_DIM), F32),
            pltpu.VMEM((ns, tt, d), BF16),
            pltpu.VMEM((ns, tt, d), BF16),
            pltpu.VMEM((ns, tt, d), BF16),
            pltpu.VMEM((ns, tt, d), BF16),
            pltpu.VMEM((ns, tt, d), BF16),
            pltpu.VMEM((ns, tt, d), F32),
            pltpu.VMEM((ns, 3, tt, d), F32),
            pltpu.VMEM((ns, HEAD_DIM, d), F32),
            pltpu.VMEM((ns, tt, d), F32),
            pltpu.VMEM((ns, tt, d), F32),
            pltpu.VMEM((ns, tt, d), F32),
        ],
        compiler_params=pltpu.CompilerParams(
            dimension_semantics=("parallel", "arbitrary"),
            vmem_limit_bytes=VMEM_LIMIT),
        name="hgrn_mixer",
    )(x, mod, gain, lb_logits, _chunk_tri(tt), w_in, hgain, w_out)


def kernel(x, c, norm_mix_gain, norm_ffn_gain, ada_w, ada_b, pool_w, pool_scale, hgrn_w_in,
           hgrn_lb_logits, hgrn_norm_gain, hgrn_w_out, ffn_w_in, ffn_w_out, final_gain):
    depth = ada_w.shape[0]
    bsz, _, d = x.shape
    n_mixers = 2
    mod = _ada_mod(c, ada_w, ada_b).reshape(depth, bsz, 6, d)

    def rows(a):
        return a.reshape(a.shape[0], 1, d)

    mix_gain, ffn_gain = rows(norm_mix_gain), rows(norm_ffn_gain)
    pool_scale, hgrn_gain = rows(pool_scale), rows(hgrn_norm_gain)
    pool_w, hgrn_w_in, hgrn_w_out, ffn_w_in, ffn_w_out = (
        w.astype(BF16) for w in (pool_w, hgrn_w_in, hgrn_w_out, ffn_w_in, ffn_w_out))
    fgain = final_gain.reshape(1, d)
    for i in range(depth):
        j = i // n_mixers
        pool = None
        if i % n_mixers == 0:
            pool = (mix_gain, pool_w, pool_scale, j)
        else:
            x = _hgrn_layer(x, mod, i, mix_gain, hgrn_lb_logits, j, hgrn_w_in, hgrn_gain, hgrn_w_out)
        x = _ffn_layer(x, mod, i, ffn_gain, ffn_w_in, ffn_w_out, fgain, final=(i == depth - 1),
                       pool=pool)
    return x
```

```python
import functools

import jax
import jax.numpy as jnp
from jax import lax
from jax.experimental import pallas as pl
from jax.experimental.pallas import tpu as pltpu

EPS = 1e-6
POOL_WINDOWS = (2, 4, 8, 16)
POOL_HALO = 16
POOL_BLOCK = 128
POOL_BAND = 256
SUBLANES = 8
HEAD_DIM = 128
CHUNK = 64
SUPER = 128
MAX_SPAN = 64.0
HGRN_STREAMS = 2
VMEM_LIMIT = 56 * 1024 * 1024

F32 = jnp.float32
BF16 = jnp.bfloat16


def _dot(a, b):
    return jnp.dot(a, b, preferred_element_type=F32)


def _dot_nt(a, b):
    return lax.dot_general(a, b, (((1,), (1,)), ((), ())), preferred_element_type=F32)


def _dot_tn(a, b):
    return lax.dot_general(a, b, (((0,), (0,)), ((), ())), preferred_element_type=F32)


def _const_spec(shape):
    nd = len(shape)
    return pl.BlockSpec(shape, lambda *_: (0,) * nd)


def _layer_spec(arr, layer):
    nd = arr.ndim - 1
    return pl.BlockSpec((None,) + arr.shape[1:], lambda *_: (layer,) + (0,) * nd)


def _norm_mod(x, gain, shift, scale):
    ms = jnp.mean(x * x, axis=-1, keepdims=True)
    return (x * lax.rsqrt(ms + EPS)) * (gain * (1.0 + scale)) + shift


def _ada_kernel(c_ref, w_ref, b_ref, o_ref):
    c = c_ref[...]
    ca = (c * jax.nn.sigmoid(c)).astype(BF16)
    o_ref[...] = _dot(ca, w_ref[...].astype(BF16)) + b_ref[...]


def _ada_mod(c, ada_w, ada_b):
    depth, d, n = ada_w.shape
    bsz = c.shape[0]
    tn = n // 4
    return pl.pallas_call(
        _ada_kernel,
        grid=(depth, n // tn),
        in_specs=[
            pl.BlockSpec((bsz, d), lambda l, j: (0, 0)),
            pl.BlockSpec((None, d, tn), lambda l, j: (l, 0, j)),
            pl.BlockSpec((None, 1, tn), lambda l, j: (l, 0, j)),
        ],
        out_specs=pl.BlockSpec((None, bsz, tn), lambda l, j: (l, 0, j)),
        out_shape=jax.ShapeDtypeStruct((depth, bsz, n), F32),
        compiler_params=pltpu.CompilerParams(
            dimension_semantics=("arbitrary", "arbitrary"),
            vmem_limit_bytes=VMEM_LIMIT),
        name="ada_mod",
    )(c, ada_w, ada_b.reshape(depth, 1, n))


def _pool_bands():
    i = jnp.arange(POOL_BLOCK)[:, None] + (POOL_BAND - POOL_BLOCK)
    j = jnp.arange(POOL_BAND)[None, :]
    return jnp.stack([((j <= i) & (j > i - w)) for w in POOL_WINDOWS]).astype(BF16)


def _split_bf16(a):
    hi = a.astype(BF16)
    return hi, (a - hi.astype(F32)).astype(BF16)


def _ffn_kernel(*refs, fc, final, pool, n_sub):
    if pool:
        (x_ref, halo_ref, mod_ref, mgain_ref, band_ref, pw_ref, ps_ref, gain_ref, win_ref,
         wout_ref, fgain_ref, o_ref, act_ref, hf_ref, hi_ref, lo_ref, d_ref) = refs
    else:
        x_ref, mod_ref, gain_ref, win_ref, wout_ref, fgain_ref, o_ref, act_ref = refs
    i = pl.program_id(1)
    tm, d = x_ref.shape
    f = wout_ref.shape[0]
    ts = tm // n_sub
    m = mod_ref[...]
    pad = POOL_BAND - POOL_BLOCK

    if pool:
        gw = d // len(POOL_WINDOWS)
        hh = jnp.where(i == 0, 0.0, _norm_mod(halo_ref[...], mgain_ref[...], m[0:1], m[1:2]))
        for ref, top in zip((hi_ref, lo_ref), _split_bf16(hh)):
            ref[0:pad - POOL_HALO, :] = jnp.zeros((pad - POOL_HALO, d), BF16)
            ref[pad - POOL_HALO:pad, :] = top

    def pool_split(rows):
        h = _norm_mod(x_ref[rows, :], mgain_ref[...], m[0:1], m[1:2])
        hf_ref[...] = h
        for ref, piece in zip((hi_ref, lo_ref), _split_bf16(h)):
            ref[pad + rows.start:pad + rows.stop, :] = piece

    def pool_sums(rows):
        n = rows.stop - rows.start
        pos = i * tm + rows.start + lax.broadcasted_iota(jnp.int32, (n, 1), 0)
        for g, w in enumerate(POOL_WINDOWS):
            cols = slice(g * gw, (g + 1) * gw)
            inv_cnt = 1.0 / jnp.minimum(pos + 1, w).astype(F32)
            for blk in range(n // POOL_BLOCK):
                r0 = rows.start + blk * POOL_BLOCK
                win = slice(r0, r0 + POOL_BAND)
                sm = _dot(band_ref[g], hi_ref[win, cols]) + _dot(band_ref[g], lo_ref[win, cols])
                loc = slice(blk * POOL_BLOCK, (blk + 1) * POOL_BLOCK)
                d_ref[loc, cols] = (sm * inv_cnt[loc] - hf_ref[loc, cols]).astype(BF16)

    def pool_mix(rows):
        for g in range(len(POOL_WINDOWS)):
            cols = slice(g * gw, (g + 1) * gw)
            y = _dot(d_ref[:, cols], pw_ref[g]) * ps_ref[:, cols]
            o_ref[rows, cols] = x_ref[rows, cols] + m[2:3, cols] * y

    src = o_ref if pool else x_ref
    sub_rows = [slice(j * ts, (j + 1) * ts) for j in range(n_sub)]

    def norm(j):
        return _norm_mod(src[sub_rows[j], :], gain_ref[...], m[3:4], m[4:5]).astype(BF16)

    if pool:
        pool_split(sub_rows[0])
        pool_sums(sub_rows[0])
        pool_mix(sub_rows[0])
    h = norm(0)
    for j in range(n_sub):
        rows = sub_rows[j]
        for c in range(f // fc):
            a = _dot(h, win_ref[:, c * fc:(c + 1) * fc])
            b = _dot(h, win_ref[:, f + c * fc:f + (c + 1) * fc])
            act_ref[rows, c * fc:(c + 1) * fc] = (a * jax.nn.sigmoid(a) * b).astype(BF16)
        if pool and j + 1 < n_sub:
            pool_split(sub_rows[j + 1])
            pool_sums(sub_rows[j + 1])
        y = _dot(act_ref[rows, :], wout_ref[...])
        out = src[rows, :] + m[5:6] * y
        if final:
            ms = jnp.mean(out * out, axis=-1, keepdims=True)
            out = out * lax.rsqrt(ms + EPS) * fgain_ref[...]
        o_ref[rows, :] = out
        if j + 1 < n_sub:
            if pool:
                pool_mix(sub_rows[j + 1])
            h = norm(j + 1)


def _ffn_layer(x, mod, layer, gain, w_in, w_out, final_gain, *, final, pool=None,
               tm=1024, fc=256, n_sub=4):
    bsz, t, d = x.shape
    f = w_out.shape[1]
    kern = functools.partial(_ffn_kernel, fc=fc, final=final, pool=pool is not None, n_sub=n_sub)
    x_spec = pl.BlockSpec((None, tm, d), lambda b, i: (b, i, 0))
    mod_spec = pl.BlockSpec((None, None, 6, d), lambda b, i: (layer, b, 0, 0))
    ffn_specs = [_layer_spec(gain, layer), _layer_spec(w_in, layer), _layer_spec(w_out, layer),
                 _const_spec((1, d))]
    ffn_args = [gain, w_in, w_out, final_gain]
    scratch = [pltpu.VMEM((tm, f), BF16)]
    if pool is None:
        in_specs = [x_spec, mod_spec] + ffn_specs
        args = [x, mod] + ffn_args
    else:
        mix_gain, pool_w, pool_scale, row = pool
        hb = tm // POOL_HALO
        bands = _pool_bands()
        halo_spec = pl.BlockSpec((None, POOL_HALO, d),
                                 lambda b, i: (b, jnp.maximum(i * hb - 1, 0), 0))
        in_specs = [x_spec, halo_spec, mod_spec, _layer_spec(mix_gain, layer),
                    _const_spec(bands.shape), _layer_spec(pool_w, row),
                    _layer_spec(pool_scale, row)] + ffn_specs
        args = [x, x, mod, mix_gain, bands, pool_w, pool_scale] + ffn_args
        ext = tm + POOL_BAND - POOL_BLOCK
        scratch += [
            pltpu.VMEM((tm // n_sub, d), F32),
            pltpu.VMEM((ext, d), BF16),
            pltpu.VMEM((ext, d), BF16),
            pltpu.VMEM((tm // n_sub, d), BF16),
        ]
    return pl.pallas_call(
        kern,
        grid=(bsz, t // tm),
        in_specs=in_specs,
        out_specs=pl.BlockSpec((None, tm, d), lambda b, i: (b, i, 0)),
        out_shape=jax.ShapeDtypeStruct(x.shape, F32),
        scratch_shapes=scratch,
        compiler_params=pltpu.CompilerParams(
            dimension_semantics=("parallel", "parallel"),
            vmem_limit_bytes=VMEM_LIMIT),
        name="pool_ffn" if pool is not None else "ffn",
    )(*args)


def _lower_bound(logits, row):
    mx = jnp.max(logits, axis=0, keepdims=True)
    e = jnp.exp(logits - mx)
    sm = e / jnp.sum(e, axis=0, keepdims=True)
    acc = sm[0:1]
    for r in range(1, row + 1):
        acc = acc + sm[r:r + 1]
    return acc - sm[0:1]


def _hgrn_kernel(x_ref, mod_ref, gain_ref, lbl_ref, tri_ref, win_ref, hgain_ref, wout_ref, o_ref,
                 st_ref, qi_ref, qa_ref, ka_ref, kh_ref, v_ref, b_ref, qk_ref, dl_ref, gate_ref,
                 oacc_ref, oin_ref, *, row):
    i = pl.program_id(1)
    ns, tt, d = x_ref.shape
    streams = range(ns)
    head_cols = [slice(hd * HEAD_DIM, (hd + 1) * HEAD_DIM) for hd in range(d // HEAD_DIM)]
    n_chunks = tt // CHUNK

    @pl.when(i == 0)
    def _():
        st_ref[...] = jnp.zeros_like(st_ref)
        dl_ref[...] = jnp.zeros_like(dl_ref)

    gain = gain_ref[...]
    lb = _lower_bound(lbl_ref[...], row)
    tri = tri_ref[...]

    def per_stream(a):
        return [a[s * tt:(s + 1) * tt] for s in streams]

    h = jnp.concatenate(
        [_norm_mod(x_ref[s], gain, mod_ref[s][0:1], mod_ref[s][1:2]).astype(BF16) for s in streams])
    z = per_stream(_dot(h, win_ref[:, d:2 * d]))
    q = per_stream(_dot(h, win_ref[:, 0:d]))

    k, g_hi, g_lo = [], [], []
    for s in streams:
        e = jnp.exp(-z[s])
        sg = 1.0 / (1.0 + e)
        logf = jnp.log(lb + (1.0 - lb) * sg)
        k.append((1.0 - lb) * (e * sg))
        hi = logf.astype(BF16)
        g_hi.append(hi)
        g_lo.append((logf - hi.astype(F32)).astype(BF16))

    b = [_dot(tri, g_hi[s]) + _dot(tri, g_lo[s]) for s in streams]

    v = per_stream(_dot(h, win_ref[:, 2 * d:3 * d]))
    og = per_stream(_dot(h, win_ref[:, 3 * d:4 * d]))

    mid = CHUNK // 2 - 1
    span = jnp.zeros((1, d), F32)
    for s in streams:
        b_ref[s] = b[s]
        qk_ref[s, 0] = q[s]
        qk_ref[s, 1] = k[s]
        qi_ref[s] = (q[s] * jnp.exp(b[s])).astype(BF16)
        for cidx in range(n_chunks):
            rows = slice(cidx * CHUNK, (cidx + 1) * CHUNK)
            bc = b[s][rows]
            bm = bc[mid:mid + 1]
            bl = bc[CHUNK - 1:CHUNK]
            qa_ref[s, rows, :] = (q[s][rows] * jnp.exp(bc - bm)).astype(BF16)
            ka_ref[s, rows, :] = (k[s][rows] * jnp.exp(bm - bc)).astype(BF16)
            kh_ref[s, rows, :] = (k[s][rows] * jnp.exp(bl - bc)).astype(BF16)
            dl_ref[s, cidx:cidx + 1, :] = jnp.exp(bl)
            span = jnp.maximum(span, jnp.maximum(-bm, bm - bl))
    fast = jnp.max(span) <= MAX_SPAN
    for s in streams:
        v_ref[s] = v[s].astype(BF16)
        qk_ref[s, 2] = v[s]
        gate_ref[s] = og[s] * jax.nn.sigmoid(og[s])

    n_blocks = tt // SUPER
    cpb = SUPER // CHUNK
    ri = lax.broadcasted_iota(jnp.int32, (SUPER, SUPER), 0)
    ci = lax.broadcasted_iota(jnp.int32, (SUPER, SUPER), 1)
    causal = (ri // CHUNK == ci // CHUNK) & (ci <= ri)
    zero_rows = jnp.zeros((CHUNK, HEAD_DIM), BF16)
    dcols = {(s, hd): dl_ref[s, :, cols].T for s in streams for hd, cols in enumerate(head_cols)}
    dst = {}

    def intra(sc, s):
        rows = slice(sc * SUPER, (sc + 1) * SUPER)
        atts = [_dot_nt(qa_ref[s, rows, cols], ka_ref[s, rows, cols]) for cols in head_cols]
        for cols, att in zip(head_cols, atts):
            att = jnp.where(causal, att, 0.0).astype(BF16)
            oin_ref[s, rows, cols] = _dot(att, v_ref[s, rows, cols])

    def updates(sc, s):
        rows = slice(sc * SUPER, (sc + 1) * SUPER)
        for hd, cols in enumerate(head_cols):
            vv = v_ref[s, rows, cols]
            spread = jnp.concatenate(
                [jnp.concatenate([vv[c * CHUNK:(c + 1) * CHUNK] if r == c else zero_rows
                                  for r in range(cpb)], axis=0) for c in range(cpb)], axis=1)
            dst[sc, s, hd] = _dot_tn(kh_ref[s, rows, cols], spread)

    def recurrence_step(cidx):
        sc, c = divmod(cidx, cpb)
        crow = slice(cidx * CHUNK, (cidx + 1) * CHUNK)
        for s in streams:
            for hd, cols in enumerate(head_cols):
                oacc_ref[s, crow, cols] = _dot(qi_ref[s, crow, cols], st_ref[s, hd].astype(BF16))
        for s in streams:
            for hd, cols in enumerate(head_cols):
                upd = dst[sc, s, hd][:, c * HEAD_DIM:(c + 1) * HEAD_DIM]
                st_ref[s, hd] = st_ref[s, hd] * dcols[s, hd][:, cidx:cidx + 1] + upd

    fillers = []
    for sc in range(n_blocks):
        if sc + 1 < n_blocks:
            fillers += [functools.partial(updates, sc + 1, s) for s in streams]
        fillers += [functools.partial(intra, sc, s) for s in streams]
    for s in streams:
        updates(0, s)
    for cidx in range(n_chunks):
        recurrence_step(cidx)
        if fillers:
            fillers.pop(0)()
        nxt = cidx + 1
        assert nxt % cpb or nxt == n_chunks or all((nxt // cpb, s, 0) in dst for s in streams)
    for filler in fillers:
        filler()

    @pl.when(jnp.logical_not(fast))
    def _():
        rowi = lax.broadcasted_iota(jnp.int32, (CHUNK, 1), 0)

        def chunk_body(it, carry):
            s, cidx = it // n_chunks, it % n_chunks
            row0 = pl.multiple_of(cidx * CHUNK, CHUNK)
            rows = pl.ds(row0, CHUNK)
            for cols in head_cols:
                bb = b_ref[s, rows, cols]
                qq = qk_ref[s, 0, rows, cols]

                def key_group(jg, acc):
                    krows = pl.ds(pl.multiple_of(row0 + jg * SUBLANES, SUBLANES), SUBLANES)
                    bk = b_ref[s, krows, cols]
                    kk = qk_ref[s, 1, krows, cols]
                    vv = qk_ref[s, 2, krows, cols]
                    for r in range(SUBLANES):
                        p = qq * kk[r:r + 1] * jnp.exp(jnp.minimum(bb - bk[r:r + 1], 0.0))
                        col = jnp.sum(p, axis=-1, keepdims=True)
                        acc = acc + jnp.where(rowi >= jg * SUBLANES + r, col, 0.0) * vv[r:r + 1]
                    return acc

                zero = jnp.zeros((CHUNK, HEAD_DIM), F32)
                oin_ref[s, rows, cols] = lax.fori_loop(0, CHUNK // SUBLANES, key_group, zero)
            return carry
        lax.fori_loop(0, ns * n_chunks, chunk_body, 0)

    hg = hgain_ref[...]
    for s in streams:
        for cols in head_cols:
            o = oacc_ref[s, :, cols] + oin_ref[s, :, cols]
            ms = jnp.mean(o * o, axis=-1, keepdims=True)
            oacc_ref[s, :, cols] = o * lax.rsqrt(ms + EPS) * hg[:, cols] * gate_ref[s, :, cols]
    y = _dot(oacc_ref[...].reshape(ns * tt, d).astype(BF16), wout_ref[...])
    for s in streams:
        o_ref[s] = x_ref[s] + mod_ref[s][2:3] * y[s * tt:(s + 1) * tt]


def _chunk_tri(tt):
    r = jnp.arange(tt)[:, None]
    c = jnp.arange(tt)[None, :]
    return ((r // CHUNK == c // CHUNK) & (c <= r)).astype(BF16)


def _hgrn_layer(x, mod, layer, gain, lb_logits, row, w_in, hgain, w_out, *, tt=256):
    bsz, t, d = x.shape
    ns = HGRN_STREAMS
    heads = d // HEAD_DIM
    n_chunks = tt // CHUNK
    kern = functools.partial(_hgrn_kernel, row=row)
    return pl.pallas_call(
        kern,
        grid=(bsz // ns, t // tt),
        in_specs=[
            pl.BlockSpec((ns, tt, d), lambda b, i: (b, i, 0)),
            pl.BlockSpec((None, ns, 6, d), lambda b, i: (layer, b, 0, 0)),
            _layer_spec(gain, layer),
            _const_spec(lb_logits.shape),
            _const_spec((tt, tt)),
            _layer_spec(w_in, row),
            _layer_spec(hgain, row),
            _layer_spec(w_out, row),
        ],
        out_specs=pl.BlockSpec((ns, tt, d), lambda b, i: (b, i, 0)),
        out_shape=jax.ShapeDtypeStruct(x.shape, F32),
        scratch_shapes=[
            pltpu.VMEM((ns, heads, HEAD_DIM, HEAD_DIM), F32),
            pltpu.VMEM((ns, tt, d), BF16),
            pltpu.VMEM((ns, tt, d), BF16),
            pltpu.VMEM((ns, tt, d), BF16),
            pltpu.VMEM((ns, tt, d), BF16),
            pltpu.VMEM((ns, tt, d), BF16),
            pltpu.VMEM((ns, tt, d), F32),
            pltpu.VMEM((ns, 3, tt, d), F32),
            pltpu.VMEM((ns, HEAD_DIM, d), F32),
            pltpu.VMEM((ns, tt, d), F32),
            pltpu.VMEM((ns, tt, d), F32),
            pltpu.VMEM((ns, tt, d), F32),
        ],
        compiler_params=pltpu.CompilerParams(
            dimension_semantics=("parallel", "arbitrary"),
            vmem_limit_bytes=VMEM_LIMIT),
        name="hgrn_mixer",
    )(x, mod, gain, lb_logits, _chunk_tri(tt), w_in, hgain, w_out)


def kernel(x, c, norm_mix_gain, norm_ffn_gain, ada_w, ada_b, pool_w, pool_scale, hgrn_w_in,
           hgrn_lb_logits, hgrn_norm_gain, hgrn_w_out, ffn_w_in, ffn_w_out, final_gain):
    depth = ada_w.shape[0]
    bsz, _, d = x.shape
    n_mixers = 2
    mod = _ada_mod(c, ada_w, ada_b).reshape(depth, bsz, 6, d)

    def rows(a):
        return a.reshape(a.shape[0], 1, d)

    mix_gain, ffn_gain = rows(norm_mix_gain), rows(norm_ffn_gain)
    pool_scale, hgrn_gain = rows(pool_scale), rows(hgrn_norm_gain)
    pool_w, hgrn_w_in, hgrn_w_out, ffn_w_in, ffn_w_out = (
        w.astype(BF16) for w in (pool_w, hgrn_w_in, hgrn_w_out, ffn_w_in, ffn_w_out))
    fgain = final_gain.reshape(1, d)
    for i in range(depth):
        j = i // n_mixers
        pool = None
        if i % n_mixers == 0:
            pool = (mix_gain, pool_w, pool_scale, j)
        else:
            x = _hgrn_layer(x, mod, i, mix_gain, hgrn_lb_logits, j, hgrn_w_in, hgrn_gain, hgrn_w_out)
        x = _ffn_layer(x, mod, i, ffn_gain, ffn_w_in, ffn_w_out, fgain, final=(i == depth - 1),
                       pool=pool)
    return x
```

```python
import functools

import jax
import jax.numpy as jnp
from jax import lax
from jax.experimental import pallas as pl
from jax.experimental.pallas import tpu as pltpu

EPS = 1e-6
POOL_WINDOWS = (2, 4, 8, 16)
POOL_HALO = 16
POOL_BLOCK = 128
POOL_BAND = 256
SUBLANES = 8
HEAD_DIM = 128
CHUNK = 64
SUPER = 128
MAX_SPAN = 64.0
HGRN_STREAMS = 2
VMEM_LIMIT = 56 * 1024 * 1024

F32 = jnp.float32
BF16 = jnp.bfloat16


def _dot(a, b):
    return jnp.dot(a, b, preferred_element_type=F32)


def _dot_nt(a, b):
    return lax.dot_general(a, b, (((1,), (1,)), ((), ())), preferred_element_type=F32)


def _dot_tn(a, b):
    return lax.dot_general(a, b, (((0,), (0,)), ((), ())), preferred_element_type=F32)


def _const_spec(shape):
    nd = len(shape)
    return pl.BlockSpec(shape, lambda *_: (0,) * nd)


def _layer_spec(arr, layer):
    nd = arr.ndim - 1
    return pl.BlockSpec((None,) + arr.shape[1:], lambda *_: (layer,) + (0,) * nd)


def _norm_mod(x, gain, shift, scale):
    ms = jnp.mean(x * x, axis=-1, keepdims=True)
    return (x * lax.rsqrt(ms + EPS)) * (gain * (1.0 + scale)) + shift


def _ada_kernel(c_ref, w_ref, b_ref, o_ref):
    c = c_ref[...]
    ca = (c * jax.nn.sigmoid(c)).astype(BF16)
    o_ref[...] = _dot(ca, w_ref[...].astype(BF16)) + b_ref[...]


def _ada_mod(c, ada_w, ada_b):
    depth, d, n = ada_w.shape
    bsz = c.shape[0]
    tn = n // 4
    return pl.pallas_call(
        _ada_kernel,
        grid=(depth, n // tn),
        in_specs=[
            pl.BlockSpec((bsz, d), lambda l, j: (0, 0)),
            pl.BlockSpec((None, d, tn), lambda l, j: (l, 0, j)),
            pl.BlockSpec((None, 1, tn), lambda l, j: (l, 0, j)),
        ],
        out_specs=pl.BlockSpec((None, bsz, tn), lambda l, j: (l, 0, j)),
        out_shape=jax.ShapeDtypeStruct((depth, bsz, n), F32),
        compiler_params=pltpu.CompilerParams(
            dimension_semantics=("arbitrary", "arbitrary"),
            vmem_limit_bytes=VMEM_LIMIT),
        name="ada_mod",
    )(c, ada_w, ada_b.reshape(depth, 1, n))


def _pool_bands():
    i = jnp.arange(POOL_BLOCK)[:, None] + (POOL_BAND - POOL_BLOCK)
    j = jnp.arange(POOL_BAND)[None, :]
    return jnp.stack([((j <= i) & (j > i - w)) for w in POOL_WINDOWS]).astype(BF16)


def _split_bf16(a):
    hi = a.astype(BF16)
    return hi, (a - hi.astype(F32)).astype(BF16)


def _ffn_kernel(*refs, fc, final, pool, n_sub):
    if pool:
        (x_ref, halo_ref, mod_ref, mgain_ref, band_ref, pw_ref, ps_ref, gain_ref, win_ref,
         wout_ref, fgain_ref, o_ref, act_ref, hf_ref, hi_ref, lo_ref, d_ref) = refs
    else:
        x_ref, mod_ref, gain_ref, win_ref, wout_ref, fgain_ref, o_ref, act_ref = refs
    i = pl.program_id(1)
    tm, d = x_ref.shape
    f = wout_ref.shape[0]
    ts = tm // n_sub
    m = mod_ref[...]
    pad = POOL_BAND - POOL_BLOCK

    if pool:
        gw = d // len(POOL_WINDOWS)
        hh = jnp.where(i == 0, 0.0, _norm_mod(halo_ref[...], mgain_ref[...], m[0:1], m[1:2]))
        for ref, top in zip((hi_ref, lo_ref), _split_bf16(hh)):
            ref[0:pad - POOL_HALO, :] = jnp.zeros((pad - POOL_HALO, d), BF16)
            ref[pad - POOL_HALO:pad, :] = top

    def pool_split(rows):
        h = _norm_mod(x_ref[rows, :], mgain_ref[...], m[0:1], m[1:2])
        hf_ref[...] = h
        for ref, piece in zip((hi_ref, lo_ref), _split_bf16(h)):
            ref[pad + rows.start:pad + rows.stop, :] = piece

    def pool_sums(rows):
        n = rows.stop - rows.start
        pos = i * tm + rows.start + lax.broadcasted_iota(jnp.int32, (n, 1), 0)
        for g, w in enumerate(POOL_WINDOWS):
            cols = slice(g * gw, (g + 1) * gw)
            inv_cnt = 1.0 / jnp.minimum(pos + 1, w).astype(F32)
            for blk in range(n // POOL_BLOCK):
                r0 = rows.start + blk * POOL_BLOCK
                win = slice(r0, r0 + POOL_BAND)
                sm = _dot(band_ref[g], hi_ref[win, cols]) + _dot(band_ref[g], lo_ref[win, cols])
                loc = slice(blk * POOL_BLOCK, (blk + 1) * POOL_BLOCK)
                d_ref[loc, cols] = (sm * inv_cnt[loc] - hf_ref[loc, cols]).astype(BF16)

    def pool_mix(rows):
        for g in range(len(POOL_WINDOWS)):
            cols = slice(g * gw, (g + 1) * gw)
            y = _dot(d_ref[:, cols], pw_ref[g]) * ps_ref[:, cols]
            o_ref[rows, cols] = x_ref[rows, cols] + m[2:3, cols] * y

    src = o_ref if pool else x_ref
    sub_rows = [slice(j * ts, (j + 1) * ts) for j in range(n_sub)]

    def norm(j):
        return _norm_mod(src[sub_rows[j], :], gain_ref[...], m[3:4], m[4:5]).astype(BF16)

    if pool:
        pool_split(sub_rows[0])
        pool_sums(sub_rows[0])
        pool_mix(sub_rows[0])
    h = norm(0)
    for j in range(n_sub):
        rows = sub_rows[j]
        for c in range(f // fc):
            a = _dot(h, win_ref[:, c * fc:(c + 1) * fc])
            b = _dot(h, win_ref[:, f + c * fc:f + (c + 1) * fc])
            act_ref[rows, c * fc:(c + 1) * fc] = (a * jax.nn.sigmoid(a) * b).astype(BF16)
        if pool and j + 1 < n_sub:
            pool_split(sub_rows[j + 1])
            pool_sums(sub_rows[j + 1])
        y = _dot(act_ref[rows, :], wout_ref[...])
        out = src[rows, :] + m[5:6] * y
        if final:
            ms = jnp.mean(out * out, axis=-1, keepdims=True)
            out = out * lax.rsqrt(ms + EPS) * fgain_ref[...]
        o_ref[rows, :] = out
        if j + 1 < n_sub:
            if pool:
                pool_mix(sub_rows[j + 1])
            h = norm(j + 1)


def _ffn_layer(x, mod, layer, gain, w_in, w_out, final_gain, *, final, pool=None,
               tm=1024, fc=256, n_sub=4):
    bsz, t, d = x.shape
    f = w_out.shape[1]
    kern = functools.partial(_ffn_kernel, fc=fc, final=final, pool=pool is not None, n_sub=n_sub)
    x_spec = pl.BlockSpec((None, tm, d), lambda b, i: (b, i, 0))
    mod_spec = pl.BlockSpec((None, None, 6, d), lambda b, i: (layer, b, 0, 0))
    ffn_specs = [_layer_spec(gain, layer), _layer_spec(w_in, layer), _layer_spec(w_out, layer),
                 _const_spec((1, d))]
    ffn_args = [gain, w_in, w_out, final_gain]
    scratch = [pltpu.VMEM((tm, f), BF16)]
    if pool is None:
        in_specs = [x_spec, mod_spec] + ffn_specs
        args = [x, mod] + ffn_args
    else:
        mix_gain, pool_w, pool_scale, row = pool
        hb = tm // POOL_HALO
        bands = _pool_bands()
        halo_spec = pl.BlockSpec((None, POOL_HALO, d),
                                 lambda b, i: (b, jnp.maximum(i * hb - 1, 0), 0))
        in_specs = [x_spec, halo_spec, mod_spec, _layer_spec(mix_gain, layer),
                    _const_spec(bands.shape), _layer_spec(pool_w, row),
                    _layer_spec(pool_scale, row)] + ffn_specs
        args = [x, x, mod, mix_gain, bands, pool_w, pool_scale] + ffn_args
        ext = tm + POOL_BAND - POOL_BLOCK
        scratch += [
            pltpu.VMEM((tm // n_sub, d), F32),
            pltpu.VMEM((ext, d), BF16),
            pltpu.VMEM((ext, d), BF16),
            pltpu.VMEM((tm // n_sub, d), BF16),
        ]
    return pl.pallas_call(
        kern,
        grid=(bsz, t // tm),
        in_specs=in_specs,
        out_specs=pl.BlockSpec((None, tm, d), lambda b, i: (b, i, 0)),
        out_shape=jax.ShapeDtypeStruct(x.shape, F32),
        scratch_shapes=scratch,
        compiler_params=pltpu.CompilerParams(
            dimension_semantics=("parallel", "parallel"),
            vmem_limit_bytes=VMEM_LIMIT),
        name="pool_ffn" if pool is not None else "ffn",
    )(*args)


def _lower_bound(logits, row):
    mx = jnp.max(logits, axis=0, keepdims=True)
    e = jnp.exp(logits - mx)
    sm = e / jnp.sum(e, axis=0, keepdims=True)
    acc = sm[0:1]
    for r in range(1, row + 1):
        acc = acc + sm[r:r + 1]
    return acc - sm[0:1]


def _hgrn_kernel(x_ref, mod_ref, gain_ref, lbl_ref, tri_ref, win_ref, hgain_ref, wout_ref, o_ref,
                 st_ref, qi_ref, qa_ref, ka_ref, kh_ref, v_ref, b_ref, qk_ref, dl_ref, gate_ref,
                 oacc_ref, oin_ref, *, row):
    i = pl.program_id(1)
    ns, tt, d = x_ref.shape
    streams = range(ns)
    head_cols = [slice(hd * HEAD_DIM, (hd + 1) * HEAD_DIM) for hd in range(d // HEAD_DIM)]
    n_chunks = tt // CHUNK

    @pl.when(i == 0)
    def _():
        st_ref[...] = jnp.zeros_like(st_ref)
        dl_ref[...] = jnp.zeros_like(dl_ref)

    gain = gain_ref[...]
    lb = _lower_bound(lbl_ref[...], row)
    tri = tri_ref[...]

    def per_stream(a):
        return [a[s * tt:(s + 1) * tt] for s in streams]

    h = jnp.concatenate(
        [_norm_mod(x_ref[s], gain, mod_ref[s][0:1], mod_ref[s][1:2]).astype(BF16) for s in streams])
    z = per_stream(_dot(h, win_ref[:, d:2 * d]))
    for s, qs in enumerate(per_stream(_dot(h, win_ref[:, 0:d]))):
        qk_ref[s, 0] = qs

    g_hi, g_lo = [], []
    for s in streams:
        e = jnp.exp(-z[s])
        sg = 1.0 / (1.0 + e)
        logf = jnp.log(lb + (1.0 - lb) * sg)
        qk_ref[s, 1] = (1.0 - lb) * (e * sg)
        hi = logf.astype(BF16)
        g_hi.append(hi)
        g_lo.append((logf - hi.astype(F32)).astype(BF16))

    for s in streams:
        b_ref[s] = _dot(tri, g_hi[s]) + _dot(tri, g_lo[s])

    v = per_stream(_dot(h, win_ref[:, 2 * d:3 * d]))
    og = per_stream(_dot(h, win_ref[:, 3 * d:4 * d]))

    mid = CHUNK // 2 - 1
    span = jnp.zeros((1, d), F32)
    for s in streams:
        for cidx in range(n_chunks):
            rows = slice(cidx * CHUNK, (cidx + 1) * CHUNK)
            bc = b_ref[s, rows, :]
            qc = qk_ref[s, 0, rows, :]
            kc = qk_ref[s, 1, rows, :]
            bm = bc[mid:mid + 1]
            bl = bc[CHUNK - 1:CHUNK]
            qi_ref[s, rows, :] = (qc * jnp.exp(bc)).astype(BF16)
            qa_ref[s, rows, :] = (qc * jnp.exp(bc - bm)).astype(BF16)
            ka_ref[s, rows, :] = (kc * jnp.exp(bm - bc)).astype(BF16)
            kh_ref[s, rows, :] = (kc * jnp.exp(bl - bc)).astype(BF16)
            dl_ref[s, cidx:cidx + 1, :] = jnp.exp(bl)
            span = jnp.maximum(span, jnp.maximum(-bm, bm - bl))
    fast = jnp.max(span) <= MAX_SPAN
    for s in streams:
        v_ref[s] = v[s].astype(BF16)
        qk_ref[s, 2] = v[s]
        gate_ref[s] = og[s] * jax.nn.sigmoid(og[s])

    n_blocks = tt // SUPER
    cpb = SUPER // CHUNK
    ri = lax.broadcasted_iota(jnp.int32, (SUPER, SUPER), 0)
    ci = lax.broadcasted_iota(jnp.int32, (SUPER, SUPER), 1)
    causal = (ri // CHUNK == ci // CHUNK) & (ci <= ri)
    zero_rows = jnp.zeros((CHUNK, HEAD_DIM), BF16)
    dcols = {(s, hd): dl_ref[s, :, cols].T for s in streams for hd, cols in enumerate(head_cols)}
    dst = {}

    def intra(sc, s):
        rows = slice(sc * SUPER, (sc + 1) * SUPER)
        atts = [_dot_nt(qa_ref[s, rows, cols], ka_ref[s, rows, cols]) for cols in head_cols]
        for cols, att in zip(head_cols, atts):
            att = jnp.where(causal, att, 0.0).astype(BF16)
            oin_ref[s, rows, cols] = _dot(att, v_ref[s, rows, cols])

    def updates(sc, s):
        rows = slice(sc * SUPER, (sc + 1) * SUPER)
        for hd, cols in enumerate(head_cols):
            vv = v_ref[s, rows, cols]
            spread = jnp.concatenate(
                [jnp.concatenate([vv[c * CHUNK:(c + 1) * CHUNK] if r == c else zero_rows
                                  for r in range(cpb)], axis=0) for c in range(cpb)], axis=1)
            dst[sc, s, hd] = _dot_tn(kh_ref[s, rows, cols], spread)

    def recurrence_step(cidx):
        sc, c = divmod(cidx, cpb)
        crow = slice(cidx * CHUNK, (cidx + 1) * CHUNK)
        for s in streams:
            for hd, cols in enumerate(head_cols):
                oacc_ref[s, crow, cols] = _dot(qi_ref[s, crow, cols], st_ref[s, hd].astype(BF16))
        for s in streams:
            for hd, cols in enumerate(head_cols):
                upd = dst[sc, s, hd][:, c * HEAD_DIM:(c + 1) * HEAD_DIM]
                st_ref[s, hd] = st_ref[s, hd] * dcols[s, hd][:, cidx:cidx + 1] + upd

    fillers = []
    for sc in range(n_blocks):
        if sc + 1 < n_blocks:
            fillers += [functools.partial(updates, sc + 1, s) for s in streams]
        fillers += [functools.partial(intra, sc, s) for s in streams]
    for s in streams:
        updates(0, s)
    for cidx in range(n_chunks):
        recurrence_step(cidx)
        if fillers:
            fillers.pop(0)()
        nxt = cidx + 1
        assert nxt % cpb or nxt == n_chunks or all((nxt // cpb, s, 0) in dst for s in streams)
    for filler in fillers:
        filler()

    @pl.when(jnp.logical_not(fast))
    def _():
        rowi = lax.broadcasted_iota(jnp.int32, (CHUNK, 1), 0)

        def chunk_body(it, carry):
            s, cidx = it // n_chunks, it % n_chunks
            row0 = pl.multiple_of(cidx * CHUNK, CHUNK)
            rows = pl.ds(row0, CHUNK)
            for cols in head_cols:
                bb = b_ref[s, rows, cols]
                qq = qk_ref[s, 0, rows, cols]

                def key_group(jg, acc):
                    krows = pl.ds(pl.multiple_of(row0 + jg * SUBLANES, SUBLANES), SUBLANES)
                    bk = b_ref[s, krows, cols]
                    kk = qk_ref[s, 1, krows, cols]
                    vv = qk_ref[s, 2, krows, cols]
                    for r in range(SUBLANES):
                        p = qq * kk[r:r + 1] * jnp.exp(jnp.minimum(bb - bk[r:r + 1], 0.0))
                        col = jnp.sum(p, axis=-1, keepdims=True)
                        acc = acc + jnp.where(rowi >= jg * SUBLANES + r, col, 0.0) * vv[r:r + 1]
                    return acc

                zero = jnp.zeros((CHUNK, HEAD_DIM), F32)
                oin_ref[s, rows, cols] = lax.fori_loop(0, CHUNK // SUBLANES, key_group, zero)
            return carry
        lax.fori_loop(0, ns * n_chunks, chunk_body, 0)

    hg = hgain_ref[...]
    for s in streams:
        for cols in head_cols:
            o = oacc_ref[s, :, cols] + oin_ref[s, :, cols]
            ms = jnp.mean(o * o, axis=-1, keepdims=True)
            oacc_ref[s, :, cols] = o * lax.rsqrt(ms + EPS) * hg[:, cols] * gate_ref[s, :, cols]
    y = _dot(oacc_ref[...].reshape(ns * tt, d).astype(BF16), wout_ref[...])
    for s in streams:
        o_ref[s] = x_ref[s] + mod_ref[s][2:3] * y[s * tt:(s + 1) * tt]


def _chunk_tri(tt):
    r = jnp.arange(tt)[:, None]
    c = jnp.arange(tt)[None, :]
    return ((r // CHUNK == c // CHUNK) & (c <= r)).astype(BF16)


def _hgrn_layer(x, mod, layer, gain, lb_logits, row, w_in, hgain, w_out, *, tt=256):
    bsz, t, d = x.shape
    ns = HGRN_STREAMS
    heads = d // HEAD_DIM
    n_chunks = tt // CHUNK
    kern = functools.partial(_hgrn_kernel, row=row)
    return pl.pallas_call(
        kern,
        grid=(bsz // ns, t // tt),
        in_specs=[
            pl.BlockSpec((ns, tt, d), lambda b, i: (b, i, 0)),
            pl.BlockSpec((None, ns, 6, d), lambda b, i: (layer, b, 0, 0)),
            _layer_spec(gain, layer),
            _const_spec(lb_logits.shape),
            _const_spec((tt, tt)),
            _layer_spec(w_in, row),
            _layer_spec(hgain, row),
            _layer_spec(w_out, row),
        ],
        out_specs=pl.BlockSpec((ns, tt, d), lambda b, i: (b, i, 0)),
        out_shape=jax.ShapeDtypeStruct(x.shape, F32),
        scratch_shapes=[
            pltpu.VMEM((ns, heads, HEAD_DIM, HEAD_DIM), F32),
            pltpu.VMEM((ns, tt, d), BF16),
            pltpu.VMEM((ns, tt, d), BF16),
            pltpu.VMEM((ns, tt, d), BF16),
            pltpu.VMEM((ns, tt, d), BF16),
            pltpu.VMEM((ns, tt, d), BF16),
            pltpu.VMEM((ns, tt, d), F32),
            pltpu.VMEM((ns, 3, tt, d), F32),
            pltpu.VMEM((ns, HEAD_DIM, d), F32),
            pltpu.VMEM((ns, tt, d), F32),
            pltpu.VMEM((ns, tt, d), F32),
            pltpu.VMEM((ns, tt, d), F32),
        ],
        compiler_params=pltpu.CompilerParams(
            dimension_semantics=("parallel", "arbitrary"),
            vmem_limit_bytes=VMEM_LIMIT),
        name="hgrn_mixer",
    )(x, mod, gain, lb_logits, _chunk_tri(tt), w_in, hgain, w_out)


def kernel(x, c, norm_mix_gain, norm_ffn_gain, ada_w, ada_b, pool_w, pool_scale, hgrn_w_in,
           hgrn_lb_logits, hgrn_norm_gain, hgrn_w_out, ffn_w_in, ffn_w_out, final_gain):
    depth = ada_w.shape[0]
    bsz, _, d = x.shape
    n_mixers = 2
    mod = _ada_mod(c, ada_w, ada_b).reshape(depth, bsz, 6, d)

    def rows(a):
        return a.reshape(a.shape[0], 1, d)

    mix_gain, ffn_gain = rows(norm_mix_gain), rows(norm_ffn_gain)
    pool_scale, hgrn_gain = rows(pool_scale), rows(hgrn_norm_gain)
    pool_w, hgrn_w_in, hgrn_w_out, ffn_w_in, ffn_w_out = (
        w.astype(BF16) for w in (pool_w, hgrn_w_in, hgrn_w_out, ffn_w_in, ffn_w_out))
    fgain = final_gain.reshape(1, d)
    for i in range(depth):
        j = i // n_mixers
        pool = None
        if i % n_mixers == 0:
            pool = (mix_gain, pool_w, pool_scale, j)
        else:
            x = _hgrn_layer(x, mod, i, mix_gain, hgrn_lb_logits, j, hgrn_w_in, hgrn_gain, hgrn_w_out)
        x = _ffn_layer(x, mod, i, ffn_gain, ffn_w_in, ffn_w_out, fgain, final=(i == depth - 1),
                       pool=pool)
    return x
```

```python
import functools

import jax
import jax.numpy as jnp
from jax import lax
from jax.experimental import pallas as pl
from jax.experimental.pallas import tpu as pltpu

EPS = 1e-6
POOL_WINDOWS = (2, 4, 8, 16)
POOL_HALO = 16
POOL_BLOCK = 128
POOL_BAND = 256
SUBLANES = 8
HEAD_DIM = 128
CHUNK = 64
SUPER = 128
MAX_SPAN = 64.0
HGRN_STREAMS = 2
VMEM_LIMIT = 56 * 1024 * 1024

F32 = jnp.float32
BF16 = jnp.bfloat16


def _dot(a, b):
    return jnp.dot(a, b, preferred_element_type=F32)


def _dot_nt(a, b):
    return lax.dot_general(a, b, (((1,), (1,)), ((), ())), preferred_element_type=F32)


def _dot_tn(a, b):
    return lax.dot_general(a, b, (((0,), (0,)), ((), ())), preferred_element_type=F32)


def _const_spec(shape):
    nd = len(shape)
    return pl.BlockSpec(shape, lambda *_: (0,) * nd)


def _layer_spec(arr, layer):
    nd = arr.ndim - 1
    return pl.BlockSpec((None,) + arr.shape[1:], lambda *_: (layer,) + (0,) * nd)


def _silu(x):
    half = 0.5 * x
    return half + half * jnp.tanh(half)


def _norm_mod(x, gain, shift, scale):
    ms = jnp.mean(x * x, axis=-1, keepdims=True)
    return (x * lax.rsqrt(ms + EPS)) * (gain * (1.0 + scale)) + shift


def _ada_kernel(c_ref, w_ref, b_ref, o_ref):
    c = c_ref[...]
    ca = (c * jax.nn.sigmoid(c)).astype(BF16)
    o_ref[...] = _dot(ca, w_ref[...].astype(BF16)) + b_ref[...]


def _ada_mod(c, ada_w, ada_b):
    depth, d, n = ada_w.shape
    bsz = c.shape[0]
    tn = n // 4
    return pl.pallas_call(
        _ada_kernel,
        grid=(depth, n // tn),
        in_specs=[
            pl.BlockSpec((bsz, d), lambda l, j: (0, 0)),
            pl.BlockSpec((None, d, tn), lambda l, j: (l, 0, j)),
            pl.BlockSpec((None, 1, tn), lambda l, j: (l, 0, j)),
        ],
        out_specs=pl.BlockSpec((None, bsz, tn), lambda l, j: (l, 0, j)),
        out_shape=jax.ShapeDtypeStruct((depth, bsz, n), F32),
        compiler_params=pltpu.CompilerParams(
            dimension_semantics=("arbitrary", "arbitrary"),
            vmem_limit_bytes=VMEM_LIMIT),
        name="ada_mod",
    )(c, ada_w, ada_b.reshape(depth, 1, n))


def _pool_bands():
    i = jnp.arange(POOL_BLOCK)[:, None] + (POOL_BAND - POOL_BLOCK)
    j = jnp.arange(POOL_BAND)[None, :]
    return jnp.stack([((j <= i) & (j > i - w)) for w in POOL_WINDOWS]).astype(BF16)


def _split_bf16(a):
    hi = a.astype(BF16)
    return hi, (a - hi.astype(F32)).astype(BF16)


def _ffn_kernel(*refs, fc, final, pool, n_sub):
    if pool:
        (x_ref, halo_ref, mod_ref, mgain_ref, band_ref, pw_ref, ps_ref, gain_ref, win_ref,
         wout_ref, fgain_ref, o_ref, act_ref, hf_ref, hi_ref, lo_ref, d_ref) = refs
    else:
        x_ref, mod_ref, gain_ref, win_ref, wout_ref, fgain_ref, o_ref, act_ref = refs
    i = pl.program_id(1)
    tm, d = x_ref.shape
    f = wout_ref.shape[0]
    ts = tm // n_sub
    m = mod_ref[...]
    pad = POOL_BAND - POOL_BLOCK

    if pool:
        gw = d // len(POOL_WINDOWS)
        hh = jnp.where(i == 0, 0.0, _norm_mod(halo_ref[...], mgain_ref[...], m[0:1], m[1:2]))
        for ref, top in zip((hi_ref, lo_ref), _split_bf16(hh)):
            ref[0:pad - POOL_HALO, :] = jnp.zeros((pad - POOL_HALO, d), BF16)
            ref[pad - POOL_HALO:pad, :] = top

    def pool_split(rows):
        h = _norm_mod(x_ref[rows, :], mgain_ref[...], m[0:1], m[1:2])
        hf_ref[...] = h
        for ref, piece in zip((hi_ref, lo_ref), _split_bf16(h)):
            ref[pad + rows.start:pad + rows.stop, :] = piece

    def pool_sums(rows):
        n = rows.stop - rows.start
        pos = i * tm + rows.start + lax.broadcasted_iota(jnp.int32, (n, 1), 0)
        for g, w in enumerate(POOL_WINDOWS):
            cols = slice(g * gw, (g + 1) * gw)
            inv_cnt = 1.0 / jnp.minimum(pos + 1, w).astype(F32)
            for blk in range(n // POOL_BLOCK):
                r0 = rows.start + blk * POOL_BLOCK
                win = slice(r0, r0 + POOL_BAND)
                sm = _dot(band_ref[g], hi_ref[win, cols]) + _dot(band_ref[g], lo_ref[win, cols])
                loc = slice(blk * POOL_BLOCK, (blk + 1) * POOL_BLOCK)
                d_ref[loc, cols] = (sm * inv_cnt[loc] - hf_ref[loc, cols]).astype(BF16)

    def pool_mix(rows):
        for g in range(len(POOL_WINDOWS)):
            cols = slice(g * gw, (g + 1) * gw)
            y = _dot(d_ref[:, cols], pw_ref[g]) * ps_ref[:, cols]
            o_ref[rows, cols] = x_ref[rows, cols] + m[2:3, cols] * y

    src = o_ref if pool else x_ref
    sub_rows = [slice(j * ts, (j + 1) * ts) for j in range(n_sub)]

    def norm(j):
        return _norm_mod(src[sub_rows[j], :], gain_ref[...], m[3:4], m[4:5]).astype(BF16)

    if pool:
        pool_split(sub_rows[0])
        pool_sums(sub_rows[0])
        pool_mix(sub_rows[0])
    h = norm(0)
    for j in range(n_sub):
        rows = sub_rows[j]
        for c in range(f // fc):
            a = _dot(h, win_ref[:, c * fc:(c + 1) * fc])
            b = _dot(h, win_ref[:, f + c * fc:f + (c + 1) * fc])
            act_ref[rows, c * fc:(c + 1) * fc] = (_silu(a) * b).astype(BF16)
        if pool and j + 1 < n_sub:
            pool_split(sub_rows[j + 1])
            pool_sums(sub_rows[j + 1])
        y = _dot(act_ref[rows, :], wout_ref[...])
        out = src[rows, :] + m[5:6] * y
        if final:
            ms = jnp.mean(out * out, axis=-1, keepdims=True)
            out = out * lax.rsqrt(ms + EPS) * fgain_ref[...]
        o_ref[rows, :] = out
        if j + 1 < n_sub:
            if pool:
                pool_mix(sub_rows[j + 1])
            h = norm(j + 1)


def _ffn_layer(x, mod, layer, gain, w_in, w_out, final_gain, *, final, pool=None,
               tm=1024, fc=256, n_sub=4):
    bsz, t, d = x.shape
    f = w_out.shape[1]
    kern = functools.partial(_ffn_kernel, fc=fc, final=final, pool=pool is not None, n_sub=n_sub)
    x_spec = pl.BlockSpec((None, tm, d), lambda b, i: (b, i, 0))
    mod_spec = pl.BlockSpec((None, None, 6, d), lambda b, i: (layer, b, 0, 0))
    ffn_specs = [_layer_spec(gain, layer), _layer_spec(w_in, layer), _layer_spec(w_out, layer),
                 _const_spec((1, d))]
    ffn_args = [gain, w_in, w_out, final_gain]
    scratch = [pltpu.VMEM((tm, f), BF16)]
    if pool is None:
        in_specs = [x_spec, mod_spec] + ffn_specs
        args = [x, mod] + ffn_args
    else:
        mix_gain, pool_w, pool_scale, row = pool
        hb = tm // POOL_HALO
        bands = _pool_bands()
        halo_spec = pl.BlockSpec((None, POOL_HALO, d),
                                 lambda b, i: (b, jnp.maximum(i * hb - 1, 0), 0))
        in_specs = [x_spec, halo_spec, mod_spec, _layer_spec(mix_gain, layer),
                    _const_spec(bands.shape), _layer_spec(pool_w, row),
                    _layer_spec(pool_scale, row)] + ffn_specs
        args = [x, x, mod, mix_gain, bands, pool_w, pool_scale] + ffn_args
        ext = tm + POOL_BAND - POOL_BLOCK
        scratch += [
            pltpu.VMEM((tm // n_sub, d), F32),
            pltpu.VMEM((ext, d), BF16),
            pltpu.VMEM((ext, d), BF16),
            pltpu.VMEM((tm // n_sub, d), BF16),
        ]
    return pl.pallas_call(
        kern,
        grid=(bsz, t // tm),
        in_specs=in_specs,
        out_specs=pl.BlockSpec((None, tm, d), lambda b, i: (b, i, 0)),
        out_shape=jax.ShapeDtypeStruct(x.shape, F32),
        scratch_shapes=scratch,
        compiler_params=pltpu.CompilerParams(
            dimension_semantics=("parallel", "parallel"),
            vmem_limit_bytes=VMEM_LIMIT),
        name="pool_ffn" if pool is not None else "ffn",
    )(*args)


def _lower_bound(logits, row):
    mx = jnp.max(logits, axis=0, keepdims=True)
    e = jnp.exp(logits - mx)
    sm = e / jnp.sum(e, axis=0, keepdims=True)
    acc = sm[0:1]
    for r in range(1, row + 1):
        acc = acc + sm[r:r + 1]
    return acc - sm[0:1]


def _hgrn_kernel(x_ref, mod_ref, gain_ref, lbl_ref, tri_ref, win_ref, hgain_ref, wout_ref, o_ref,
                 st_ref, qi_ref, qa_ref, ka_ref, kh_ref, v_ref, b_ref, qk_ref, dl_ref, gate_ref,
                 oacc_ref, oin_ref, *, row):
    i = pl.program_id(1)
    ns, tt, d = x_ref.shape
    streams = range(ns)
    head_cols = [slice(hd * HEAD_DIM, (hd + 1) * HEAD_DIM) for hd in range(d // HEAD_DIM)]
    n_chunks = tt // CHUNK

    @pl.when(i == 0)
    def _():
        st_ref[...] = jnp.zeros_like(st_ref)
        dl_ref[...] = jnp.zeros_like(dl_ref)

    gain = gain_ref[...]
    lb = _lower_bound(lbl_ref[...], row)
    tri = tri_ref[...]

    def per_stream(a):
        return [a[s * tt:(s + 1) * tt] for s in streams]

    h = jnp.concatenate(
        [_norm_mod(x_ref[s], gain, mod_ref[s][0:1], mod_ref[s][1:2]).astype(BF16) for s in streams])
    z = per_stream(_dot(h, win_ref[:, d:2 * d]))
    q = per_stream(_dot(h, win_ref[:, 0:d]))

    k, g_hi, g_lo = [], [], []
    for s in streams:
        e = jnp.exp(-z[s])
        sg = 1.0 / (1.0 + e)
        logf = jnp.log(lb + (1.0 - lb) * sg)
        k.append((1.0 - lb) * (e * sg))
        hi = logf.astype(BF16)
        g_hi.append(hi)
        g_lo.append((logf - hi.astype(F32)).astype(BF16))

    b = [_dot(tri, g_hi[s]) + _dot(tri, g_lo[s]) for s in streams]

    v = per_stream(_dot(h, win_ref[:, 2 * d:3 * d]))
    og = per_stream(_dot(h, win_ref[:, 3 * d:4 * d]))

    mid = CHUNK // 2 - 1
    span = jnp.zeros((1, d), F32)
    for s in streams:
        b_ref[s] = b[s]
        qk_ref[s, 0] = q[s]
        qk_ref[s, 1] = k[s]
        qi_ref[s] = (q[s] * jnp.exp(b[s])).astype(BF16)
        for cidx in range(n_chunks):
            rows = slice(cidx * CHUNK, (cidx + 1) * CHUNK)
            bc = b[s][rows]
            bm = bc[mid:mid + 1]
            bl = bc[CHUNK - 1:CHUNK]
            qa_ref[s, rows, :] = (q[s][rows] * jnp.exp(bc - bm)).astype(BF16)
            ka_ref[s, rows, :] = (k[s][rows] * jnp.exp(bm - bc)).astype(BF16)
            kh_ref[s, rows, :] = (k[s][rows] * jnp.exp(bl - bc)).astype(BF16)
            dl_ref[s, cidx:cidx + 1, :] = jnp.exp(bl)
            span = jnp.maximum(span, jnp.maximum(-bm, bm - bl))
    fast = jnp.max(span) <= MAX_SPAN
    for s in streams:
        v_ref[s] = v[s].astype(BF16)
        qk_ref[s, 2] = v[s]
        gate_ref[s] = _silu(og[s])

    n_blocks = tt // SUPER
    cpb = SUPER // CHUNK
    ri = lax.broadcasted_iota(jnp.int32, (SUPER, SUPER), 0)
    ci = lax.broadcasted_iota(jnp.int32, (SUPER, SUPER), 1)
    causal = (ri // CHUNK == ci // CHUNK) & (ci <= ri)
    zero_rows = jnp.zeros((CHUNK, HEAD_DIM), BF16)
    dcols = {(s, hd): dl_ref[s, :, cols].T for s in streams for hd, cols in enumerate(head_cols)}
    dst = {}

    def intra(sc, s):
        rows = slice(sc * SUPER, (sc + 1) * SUPER)
        atts = [_dot_nt(qa_ref[s, rows, cols], ka_ref[s, rows, cols]) for cols in head_cols]
        for cols, att in zip(head_cols, atts):
            att = jnp.where(causal, att, 0.0).astype(BF16)
            oin_ref[s, rows, cols] = _dot(att, v_ref[s, rows, cols])

    def updates(sc, s):
        rows = slice(sc * SUPER, (sc + 1) * SUPER)
        for hd, cols in enumerate(head_cols):
            vv = v_ref[s, rows, cols]
            spread = jnp.concatenate(
                [jnp.concatenate([vv[c * CHUNK:(c + 1) * CHUNK] if r == c else zero_rows
                                  for r in range(cpb)], axis=0) for c in range(cpb)], axis=1)
            dst[sc, s, hd] = _dot_tn(kh_ref[s, rows, cols], spread)

    def recurrence_step(cidx):
        sc, c = divmod(cidx, cpb)
        crow = slice(cidx * CHUNK, (cidx + 1) * CHUNK)
        for s in streams:
            for hd, cols in enumerate(head_cols):
                oacc_ref[s, crow, cols] = _dot(qi_ref[s, crow, cols], st_ref[s, hd].astype(BF16))
        for s in streams:
            for hd, cols in enumerate(head_cols):
                upd = dst[sc, s, hd][:, c * HEAD_DIM:(c + 1) * HEAD_DIM]
                st_ref[s, hd] = st_ref[s, hd] * dcols[s, hd][:, cidx:cidx + 1] + upd

    fillers = []
    for sc in range(n_blocks):
        if sc + 1 < n_blocks:
            fillers += [functools.partial(updates, sc + 1, s) for s in streams]
        fillers += [functools.partial(intra, sc, s) for s in streams]
    for s in streams:
        updates(0, s)
    for cidx in range(n_chunks):
        recurrence_step(cidx)
        if fillers:
            fillers.pop(0)()
        nxt = cidx + 1
        assert nxt % cpb or nxt == n_chunks or all((nxt // cpb, s, 0) in dst for s in streams)
    for filler in fillers:
        filler()

    @pl.when(jnp.logical_not(fast))
    def _():
        rowi = lax.broadcasted_iota(jnp.int32, (CHUNK, 1), 0)

        def chunk_body(it, carry):
            s, cidx = it // n_chunks, it % n_chunks
            row0 = pl.multiple_of(cidx * CHUNK, CHUNK)
            rows = pl.ds(row0, CHUNK)
            for cols in head_cols:
                bb = b_ref[s, rows, cols]
                qq = qk_ref[s, 0, rows, cols]

                def key_group(jg, acc):
                    krows = pl.ds(pl.multiple_of(row0 + jg * SUBLANES, SUBLANES), SUBLANES)
                    bk = b_ref[s, krows, cols]
                    kk = qk_ref[s, 1, krows, cols]
                    vv = qk_ref[s, 2, krows, cols]
                    for r in range(SUBLANES):
                        p = qq * kk[r:r + 1] * jnp.exp(jnp.minimum(bb - bk[r:r + 1], 0.0))
                        col = jnp.sum(p, axis=-1, keepdims=True)
                        acc = acc + jnp.where(rowi >= jg * SUBLANES + r, col, 0.0) * vv[r:r + 1]
                    return acc

                zero = jnp.zeros((CHUNK, HEAD_DIM), F32)
                oin_ref[s, rows, cols] = lax.fori_loop(0, CHUNK // SUBLANES, key_group, zero)
            return carry
        lax.fori_loop(0, ns * n_chunks, chunk_body, 0)

    hg = hgain_ref[...]
    for s in streams:
        for cols in head_cols:
            o = oacc_ref[s, :, cols] + oin_ref[s, :, cols]
            ms = jnp.mean(o * o, axis=-1, keepdims=True)
            oacc_ref[s, :, cols] = o * lax.rsqrt(ms + EPS) * hg[:, cols] * gate_ref[s, :, cols]
    y = _dot(oacc_ref[...].reshape(ns * tt, d).astype(BF16), wout_ref[...])
    for s in streams:
        o_ref[s] = x_ref[s] + mod_ref[s][2:3] * y[s * tt:(s + 1) * tt]


def _chunk_tri(tt):
    r = jnp.arange(tt)[:, None]
    c = jnp.arange(tt)[None, :]
    return ((r // CHUNK == c // CHUNK) & (c <= r)).astype(BF16)


def _hgrn_layer(x, mod, layer, gain, lb_logits, row, w_in, hgain, w_out, *, tt=256):
    bsz, t, d = x.shape
    ns = HGRN_STREAMS
    heads = d // HEAD_DIM
    n_chunks = tt // CHUNK
    kern = functools.partial(_hgrn_kernel, row=row)
    return pl.pallas_call(
        kern,
        grid=(bsz // ns, t // tt),
        in_specs=[
            pl.BlockSpec((ns, tt, d), lambda b, i: (b, i, 0)),
            pl.BlockSpec((None, ns, 6, d), lambda b, i: (layer, b, 0, 0)),
            _layer_spec(gain, layer),
            _const_spec(lb_logits.shape),
            _const_spec((tt, tt)),
            _layer_spec(w_in, row),
            _layer_spec(hgain, row),
            _layer_spec(w_out, row),
        ],
        out_specs=pl.BlockSpec((ns, tt, d), lambda b, i: (b, i, 0)),
        out_shape=jax.ShapeDtypeStruct(x.shape, F32),
        scratch_shapes=[
            pltpu.VMEM((ns, heads, HEAD_DIM, HEAD_DIM), F32),
            pltpu.VMEM((ns, tt, d), BF16),
            pltpu.VMEM((ns, tt, d), BF16),
            pltpu.VMEM((ns, tt, d), BF16),
            pltpu.VMEM((ns, tt, d), BF16),
            pltpu.VMEM((ns, tt, d), BF16),
            pltpu.VMEM((ns, tt, d), F32),
            pltpu.VMEM((ns, 3, tt, d), F32),
            pltpu.VMEM((ns, HEAD_DIM, d), F32),
            pltpu.VMEM((ns, tt, d), F32),
            pltpu.VMEM((ns, tt, d), F32),
            pltpu.VMEM((ns, tt, d), F32),
        ],
        compiler_params=pltpu.CompilerParams(
            dimension_semantics=("parallel", "arbitrary"),
            vmem_limit_bytes=VMEM_LIMIT),
        name="hgrn_mixer",
    )(x, mod, gain, lb_logits, _chunk_tri(tt), w_in, hgain, w_out)


def kernel(x, c, norm_mix_gain, norm_ffn_gain, ada_w, ada_b, pool_w, pool_scale, hgrn_w_in,
           hgrn_lb_logits, hgrn_norm_gain, hgrn_w_out, ffn_w_in, ffn_w_out, final_gain):
    depth = ada_w.shape[0]
    bsz, _, d = x.shape
    n_mixers = 2
    mod = _ada_mod(c, ada_w, ada_b).reshape(depth, bsz, 6, d)

    def rows(a):
        return a.reshape(a.shape[0], 1, d)

    mix_gain, ffn_gain = rows(norm_mix_gain), rows(norm_ffn_gain)
    pool_scale, hgrn_gain = rows(pool_scale), rows(hgrn_norm_gain)
    pool_w, hgrn_w_in, hgrn_w_out, ffn_w_in, ffn_w_out = (
        w.astype(BF16) for w in (pool_w, hgrn_w_in, hgrn_w_out, ffn_w_in, ffn_w_out))
    fgain = final_gain.reshape(1, d)
    for i in range(depth):
        j = i // n_mixers
        pool = None
        if i % n_mixers == 0:
            pool = (mix_gain, pool_w, pool_scale, j)
        else:
            x = _hgrn_layer(x, mod, i, mix_gain, hgrn_lb_logits, j, hgrn_w_in, hgrn_gain, hgrn_w_out)
        x = _ffn_layer(x, mod, i, ffn_gain, ffn_w_in, ffn_w_out, fgain, final=(i == depth - 1),
                       pool=pool)
    return x
```
